```python
import math
import jax, jax.numpy as jnp
from jax import lax
import numpy as np

D_MODEL = 2048
BATCH = 16
SEQ = 2048
DEPTH = 2

HEAD_DIM = D_MODEL // 16
CHUNK = 128
GM_GROUPS = 4
GM_W = GM_GROUPS * HEAD_DIM
ML_HEADS = 6
ML_DK = HEAD_DIM // 2
ML_DV = HEAD_DIM
ML_W = ML_HEADS * ML_DV
ML_CONV = 4
DA_HEADS = 6
DA_DK = HEAD_DIM // 2
DA_DV = HEAD_DIM
DA_W = DA_HEADS * DA_DV
D_MIX = GM_W + ML_W + DA_W
ROPE_THETA = 500000.0
ROPE_DIM = DA_DK // 4
N_GROUPS = 4
EXP_PER_GROUP = 8
N_EXPERTS = N_GROUPS * EXP_PER_GROUP
TOP_K = 2
D_EXPERT = D_MODEL // 4
EPS = 1e-6
IN_SPLIT = (GM_W, GM_W, ML_HEADS * ML_DK, ML_HEADS * ML_DK, ML_W, ML_W, ML_HEADS, ML_HEADS, DA_HEADS * 2 * DA_DK, DA_HEADS * 2 * DA_DK, DA_W)
D_IN = 2 * GM_W + 2 * ML_HEADS * ML_DK + 2 * ML_W + 2 * ML_HEADS + 4 * DA_HEADS * DA_DK + DA_W

kernel_name = 'hybrid_gmlp_mlstm_diffattn_hiermoe'


def _rmsnorm(x, g):
    xf = x.astype(jnp.float32)
    y = xf * lax.rsqrt(jnp.mean(xf * xf, axis=-1, keepdims=True) + EPS)
    return (y * g.astype(jnp.float32)).astype(x.dtype)


def _rope_partial(x, cos, sin):
    half = ROPE_DIM // 2
    x1 = x[..., :half]
    x2 = x[..., half:ROPE_DIM]
    rest = x[..., ROPE_DIM:]
    c = cos[:, None, None, :]
    s = sin[:, None, None, :]
    return jnp.concatenate([x1 * c - x2 * s, x2 * c + x1 * s, rest], axis=-1)


def _causal_conv(x, w, b):
    c = x.shape[-1]
    y = lax.conv_general_dilated(x, w[:, None, :].astype(x.dtype), window_strides=(1,), padding=[(ML_CONV - 1, 0)], dimension_numbers=('NWC', 'WIO', 'NWC'), feature_group_count=c)
    return y + b.astype(x.dtype)


def _gmlp_chunk_mixer(u, v, vnorm_g, ws, bs):
    bsz, s, _ = u.shape
    nc = s // CHUNK
    u = jax.nn.gelu(u, approximate=False)
    v = jax.nn.gelu(v, approximate=False)
    v = _rmsnorm(v.reshape(bsz, s, GM_GROUPS, HEAD_DIM), vnorm_g.reshape(GM_GROUPS, HEAD_DIM))
    v = v.reshape(bsz, nc, CHUNK, GM_GROUPS, HEAD_DIM)
    causal = jnp.tril(jnp.ones((CHUNK, CHUNK), dtype=bool))
    ws_c = jnp.where(causal[None], ws, jnp.zeros((), ws.dtype))
    z = jnp.einsum('gts,bcsgd->bctgd', ws_c, v) + bs.T[None, None, :, :, None]
    return u * z.reshape(bsz, s, GM_W)


def _mlstm_chunk_step(carry, xs):
    c_st, n_st, m_st = carry
    q, k, v, ig, lf = xs
    b = jnp.cumsum(lf, axis=-1)
    causal = jnp.tril(jnp.ones((CHUNK, CHUNK), dtype=bool))
    d = jnp.where(causal, b[..., :, None] - b[..., None, :] + ig[..., None, :], -jnp.inf)
    inter = b + m_st[..., None]
    m_t = jnp.maximum(inter, jnp.max(d, axis=-1))
    w_intra = jnp.exp(d - m_t[..., None])
    w_inter = jnp.exp(inter - m_t)
    s = jnp.einsum('bhtd,bhsd->bhts', q, k) * w_intra
    num = jnp.einsum('bhts,bhsv->bhtv', s, v) + w_inter[..., None] * jnp.einsum('bhtd,bhdv->bhtv', q, c_st)
    den = jnp.sum(s, axis=-1) + w_inter * jnp.einsum('bhtd,bhd->bht', q, n_st)
    h = num / jnp.maximum(jnp.abs(den), jnp.exp(-m_t))[..., None]
    b_end = b[..., -1]
    w_log = b_end[..., None] - b + ig
    m_new = jnp.maximum(b_end + m_st, jnp.max(w_log, axis=-1))
    w_upd = jnp.exp(w_log - m_new[..., None])
    decay = jnp.exp(b_end + m_st - m_new)
    c_new = decay[..., None, None] * c_st + jnp.einsum('bhs,bhsd,bhsv->bhdv', w_upd, k, v)
    n_new = decay[..., None] * n_st + jnp.einsum('bhs,bhsd->bhd', w_upd, k)
    return (c_new, n_new, m_new), h


def _mlstm_mixer(q, k, v, o, ig, fg, conv_w, conv_b, ig_b, fg_b, norm_g):
    bsz, s, _ = q.shape
    nc = s // CHUNK
    f32 = jnp.float32
    qk = jax.nn.silu(_causal_conv(jnp.concatenate([q, k], axis=-1), conv_w, conv_b))
    q, k = jnp.split(qk, 2, axis=-1)

    def heads(t, dh):
        return t.astype(f32).reshape(bsz, nc, CHUNK, ML_HEADS, dh).transpose(1, 0, 3, 2, 4)

    def gates(t):
        return t.reshape(bsz, nc, CHUNK, ML_HEADS).transpose(1, 0, 3, 2)

    qc = heads(q, ML_DK)
    kc = heads(k, ML_DK) * (ML_DK ** -0.5)
    vc = heads(v, ML_DV)
    igc = gates(ig.astype(f32) + ig_b.astype(f32))
    lfc = gates(jax.nn.log_sigmoid(fg.astype(f32) + fg_b.astype(f32)))
    init = (jnp.zeros((bsz, ML_HEADS, ML_DK, ML_DV), f32), jnp.zeros((bsz, ML_HEADS, ML_DK), f32), jnp.zeros((bsz, ML_HEADS), f32))
    _, h = lax.scan(_mlstm_chunk_step, init, (qc, kc, vc, igc, lfc))
    h = h.transpose(1, 0, 3, 2, 4).reshape(bsz, s, ML_HEADS, ML_DV)
    h = _rmsnorm(h, norm_g.reshape(ML_HEADS, ML_DV)).reshape(bsz, s, ML_W)
    return (jax.nn.sigmoid(o.astype(f32)) * h).astype(o.dtype)


def _diff_attention(q, k, v, qn_g, kn_g, lam_p, subln_g, lambda_init, cos, sin):
    bsz, s, _ = q.shape
    nb = s // CHUNK
    q = _rmsnorm(q.reshape(bsz, s, DA_HEADS, 2, DA_DK), qn_g)
    k = _rmsnorm(k.reshape(bsz, s, DA_HEADS, 2, DA_DK), kn_g)
    v = v.reshape(bsz, s, DA_HEADS, DA_DV)
    q = _rope_partial(q, cos, sin) * (DA_DK ** -0.5)
    k = _rope_partial(k, cos, sin)
    lp = lam_p.astype(jnp.float32)
    lam = jnp.exp(jnp.sum(lp[0] * lp[1])) - jnp.exp(jnp.sum(lp[2] * lp[3])) + lambda_init
    qb = q.reshape(bsz, nb, CHUNK, DA_HEADS, 2, DA_DK).transpose(1, 0, 2, 3, 4, 5)
    kpos = jnp.arange(s)

    def block(args):
        qi, i = args
        sc = jnp.einsum('bqhmd,bkhmd->bhmqk', qi, k).astype(jnp.float32)
        qpos = i * CHUNK + jnp.arange(CHUNK)
        mask = kpos[None, :] <= qpos[:, None]
        a = jax.nn.softmax(jnp.where(mask, sc, -jnp.inf), axis=-1)
        a = a[:, :, 0] - lam * a[:, :, 1]
        return jnp.einsum('bhqk,bkhd->bqhd', a.astype(v.dtype), v)

    o = lax.map(block, (qb, jnp.arange(nb)))
    o = o.transpose(1, 0, 2, 3, 4).reshape(bsz, s, DA_HEADS, DA_DV)
    o = _rmsnorm(o, subln_g) * (1.0 - lambda_init)
    return o.reshape(bsz, s, DA_W)


def _hier_moe(x, w_rg, b_rg, w_re, b_re, w_gate, w_up, w_down):
    bsz, s, d = x.shape
    xt = x.reshape(-1, d)
    n = xt.shape[0]
    pg = jax.nn.softmax((xt @ w_rg + b_rg).astype(jnp.float32), axis=-1)
    pg_top, g_idx = lax.top_k(pg, 1)
    le = (xt @ w_re + b_re).astype(jnp.float32).reshape(n, N_GROUPS, EXP_PER_GROUP)
    le_g = le[jnp.arange(n), g_idx[:, 0]]
    pe = jax.nn.softmax(le_g, axis=-1)
    pe_top, e_idx = lax.top_k(pe, TOP_K)
    w = pg_top * pe_top / jnp.sum(pe_top, axis=-1, keepdims=True)
    e_glob = g_idx * EXP_PER_GROUP + e_idx
    combine = jnp.sum(jax.nn.one_hot(e_glob, N_EXPERTS, dtype=jnp.float32) * w[..., None], axis=1)

    def expert_step(acc, params):
        wg, wu, wd, c = params
        hdn = jax.nn.silu(xt @ wg) * (xt @ wu)
        return acc + c[:, None].astype(xt.dtype) * (hdn @ wd), None

    out, _ = lax.scan(expert_step, jnp.zeros_like(xt), (w_gate, w_up, w_down, combine.T))
    return out.reshape(bsz, s, d)


def setup_inputs(seed: int = 0) -> dict:
    key = jax.random.key(seed)
    ks = jax.random.split(key, 24)
    f32 = jnp.float32
    nrm = lambda k, shp, sc: sc * jax.random.normal(k, shp, f32)
    return {
        'x': jax.random.normal(ks[0], (BATCH, SEQ, D_MODEL), f32),
        'norm1_g': 1.0 + nrm(ks[1], (DEPTH, D_MODEL), 0.02),
        'w_in': nrm(ks[2], (DEPTH, D_MODEL, D_IN), D_MODEL ** -0.5),
        'gm_vnorm_g': 1.0 + nrm(ks[3], (DEPTH, GM_W), 0.02),
        'gm_ws': nrm(ks[4], (DEPTH, GM_GROUPS, CHUNK, CHUNK), 0.5 * CHUNK ** -0.5),
        'gm_b': 1.0 + nrm(ks[5], (DEPTH, GM_GROUPS, CHUNK), 0.02),
        'ml_conv_w': nrm(ks[6], (DEPTH, ML_CONV, 2 * ML_HEADS * ML_DK), ML_CONV ** -0.5),
        'ml_conv_b': nrm(ks[7], (DEPTH, 2 * ML_HEADS * ML_DK), 0.02),
        'ml_ig_b': -1.0 + nrm(ks[8], (DEPTH, ML_HEADS), 0.1),
        'ml_fg_b': 3.0 + jnp.linspace(0.0, 3.0, ML_HEADS, dtype=f32)[None, :] + nrm(ks[9], (DEPTH, ML_HEADS), 0.1),
        'ml_norm_g': 1.0 + nrm(ks[10], (DEPTH, ML_W), 0.02),
        'da_qnorm_g': 1.0 + nrm(ks[11], (DEPTH, DA_DK), 0.02),
        'da_knorm_g': 1.0 + nrm(ks[12], (DEPTH, DA_DK), 0.02),
        'da_lambda': nrm(ks[13], (DEPTH, 4, DA_DK), 0.1),
        'da_subln_g': 1.0 + nrm(ks[14], (DEPTH, DA_DV), 0.02),
        'w_out': nrm(ks[15], (DEPTH, D_MIX, D_MODEL), D_MIX ** -0.5),
        'norm2_g': 1.0 + nrm(ks[16], (DEPTH, D_MODEL), 0.02),
        'moe_w_rg': nrm(ks[17], (DEPTH, D_MODEL, N_GROUPS), D_MODEL ** -0.5),
        'moe_b_rg': nrm(ks[18], (DEPTH, N_GROUPS), 0.01),
        'moe_w_re': nrm(ks[19], (DEPTH, D_MODEL, N_EXPERTS), D_MODEL ** -0.5),
        'moe_b_re': nrm(ks[20], (DEPTH, N_EXPERTS), 0.01),
        'moe_w_gate': nrm(ks[21], (DEPTH, N_EXPERTS, D_MODEL, D_EXPERT), D_MODEL ** -0.5),
        'moe_w_up': nrm(ks[22], (DEPTH, N_EXPERTS, D_MODEL, D_EXPERT), D_MODEL ** -0.5),
        'moe_w_down': nrm(ks[23], (DEPTH, N_EXPERTS, D_EXPERT, D_MODEL), D_EXPERT ** -0.5),
    }


def reference(x, norm1_g, w_in, gm_vnorm_g, gm_ws, gm_b, ml_conv_w, ml_conv_b, ml_ig_b, ml_fg_b, ml_norm_g, da_qnorm_g, da_knorm_g, da_lambda, da_subln_g, w_out, norm2_g, moe_w_rg, moe_b_rg, moe_w_re, moe_b_re, moe_w_gate, moe_w_up, moe_w_down):
    s = x.shape[1]
    pos = jnp.arange(s, dtype=jnp.float32)
    inv_freq = ROPE_THETA ** (-jnp.arange(0, ROPE_DIM, 2, dtype=jnp.float32) / ROPE_DIM)
    ang = pos[:, None] * inv_freq[None, :]
    cos = jnp.cos(ang).astype(x.dtype)
    sin = jnp.sin(ang).astype(x.dtype)
    split_at = [int(i) for i in np.cumsum(IN_SPLIT)[:-1]]
    for l in range(DEPTH):
        lambda_init = 0.8 - 0.6 * math.exp(-0.3 * l)
        h = _rmsnorm(x, norm1_g[l])
        proj = jnp.einsum('bsd,de->bse', h, w_in[l])
        gm_u, gm_v, ml_q, ml_k, ml_v, ml_o, ml_i, ml_f, da_q, da_k, da_v = jnp.split(proj, split_at, axis=-1)
        y_gm = _gmlp_chunk_mixer(gm_u, gm_v, gm_vnorm_g[l], gm_ws[l], gm_b[l])
        y_ml = _mlstm_mixer(ml_q, ml_k, ml_v, ml_o, ml_i, ml_f, ml_conv_w[l], ml_conv_b[l], ml_ig_b[l], ml_fg_b[l], ml_norm_g[l])
        y_da = _diff_attention(da_q, da_k, da_v, da_qnorm_g[l], da_knorm_g[l], da_lambda[l], da_subln_g[l], lambda_init, cos, sin)
        mix = jnp.concatenate([y_gm, y_ml, y_da], axis=-1)
        x = x + jnp.einsum('bse,ed->bsd', mix, w_out[l])
        x = x + _hier_moe(_rmsnorm(x, norm2_g[l]), moe_w_rg[l], moe_b_rg[l], moe_w_re[l], moe_b_re[l], moe_w_gate[l], moe_w_up[l], moe_w_down[l])
    return x
```

```python
import functools
import math

import jax
import jax.numpy as jnp
from jax import lax
from jax.experimental import pallas as pl
from jax.experimental.pallas import tpu as pltpu

F32 = jnp.float32
BF16 = jnp.bfloat16
I32 = jnp.int32
U32 = jnp.uint32
HIGHEST = lax.Precision.HIGHEST

LANES = 128
D_MODEL = 2048
CHUNK = 128
GM_GROUPS = 4
GM_W = 512
ML_HEADS = 6
ML_DK = 64
ML_W = 768
ML_CONV = 4
DA_HEADS = 6
DA_DK = 64
DA_W = 768
ROPE_THETA = 500000.0
ROPE_DIM = 16
N_GROUPS = 4
EXP_PER_GROUP = 8
N_EXPERTS = 32
D_EXPERT = 512
EPS = 1e-6

COL_DA_Q, COL_DA_K, COL_DA_V = 0, 768, 1536
COL_ML_V, COL_ML_O, COL_ML_QK = 2304, 3072, 3840
COL_GM_U, COL_GM_V = 4608, 5120
P_MAIN = 5632
P_GATE = 256

VMEM_LIMIT = 56 * 1024 * 1024


def _cparams(sem):
    return pltpu.CompilerParams(dimension_semantics=sem, vmem_limit_bytes=VMEM_LIMIT)


def _gelu(x):
    return 0.5 * x * (1.0 + lax.erf(x * (1.0 / math.sqrt(2.0))))


def _rms(x, g):
    return x * lax.rsqrt(jnp.mean(x * x, axis=-1, keepdims=True) + EPS) * g


def _inproj_kernel(x_ref, g_ref, w_ref, wg_ref, o_ref, og_ref, h_ref):
    @pl.when(pl.program_id(1) == 0)
    def _():
        h = _rms(x_ref[...], g_ref[...]).astype(BF16)
        h_ref[...] = h
        og_ref[...] = jnp.dot(h, wg_ref[...], preferred_element_type=F32)

    o_ref[...] = jnp.dot(h_ref[...], w_ref[...], preferred_element_type=F32).astype(BF16)


def _inproj(x, g, w_main, w_gate, *, tm=512, tn=1408):
    n = x.shape[0]
    return pl.pallas_call(
        _inproj_kernel,
        grid=(n // tm, P_MAIN // tn),
        in_specs=[
            pl.BlockSpec((tm, D_MODEL), lambda i, j: (i, 0)),
            pl.BlockSpec((1, D_MODEL), lambda i, j: (0, 0)),
            pl.BlockSpec((D_MODEL, tn), lambda i, j: (0, j)),
            pl.BlockSpec((D_MODEL, P_GATE), lambda i, j: (0, 0)),
        ],
        out_specs=[
            pl.BlockSpec((tm, tn), lambda i, j: (i, j)),
            pl.BlockSpec((tm, P_GATE), lambda i, j: (i, 0)),
        ],
        out_shape=[
            jax.ShapeDtypeStruct((n, P_MAIN), BF16),
            jax.ShapeDtypeStruct((n, P_GATE), F32),
        ],
        scratch_shapes=[pltpu.VMEM((tm, D_MODEL), BF16)],
        compiler_params=_cparams(("parallel", "arbitrary")),
        name="inproj",
    )(x, g, w_main, w_gate)


def _gmlp_kernel(u_ref, v_ref, vg_ref, ws_ref, bt_ref, o_ref):
    tt = u_ref.shape[0]
    row = lax.broadcasted_iota(I32, (CHUNK, CHUNK), 0)
    col = lax.broadcasted_iota(I32, (CHUNK, CHUNK), 1)
    causal = col <= row
    for g in range(GM_GROUPS):
        cs = slice(g * LANES, (g + 1) * LANES)
        wsg = jnp.where(causal, ws_ref[g], 0.0).astype(BF16)
        bcol = bt_ref[:, g:g + 1]
        for c in range(tt // CHUNK):
            rs = slice(c * CHUNK, (c + 1) * CHUNK)
            v = _gelu(v_ref[rs, cs].astype(F32))
            vn = _rms(v, vg_ref[:, cs]).astype(BF16)
            z = jnp.dot(wsg, vn, preferred_element_type=F32) + bcol
            u = _gelu(u_ref[rs, cs].astype(F32))
            o_ref[rs, cs] = (u * z).astype(BF16)


def _gmlp(proj, vnorm_g, ws, b_t, *, tt=512):
    n = proj.shape[0]
    return pl.pallas_call(
        _gmlp_kernel,
        grid=(n // tt,),
        in_specs=[
            pl.BlockSpec((tt, GM_W), lambda i: (i, COL_GM_U // GM_W)),
            pl.BlockSpec((tt, GM_W), lambda i: (i, COL_GM_V // GM_W)),
            pl.BlockSpec((1, GM_W), lambda i: (0, 0)),
            pl.BlockSpec((GM_GROUPS, CHUNK, CHUNK), lambda i: (0, 0, 0)),
            pl.BlockSpec((CHUNK, GM_GROUPS), lambda i: (0, 0)),
        ],
        out_specs=pl.BlockSpec((tt, GM_W), lambda i: (i, 0)),
        out_shape=jax.ShapeDtypeStruct((n, GM_W), BF16),
        compiler_params=_cparams(("parallel",)),
        name="gmlp",
    )(proj, proj, vnorm_g, ws, b_t)


def _mlstm_kernel(qk_ref, v_ref, o_ref, gt_ref, cw_ref, cb_ref, gb_ref, ng_ref, out_ref,
                  xx_ref, c_ref, n_ref, m_ref):
    @pl.when(pl.program_id(1) == 0)
    def _():
        xx_ref[0:8, :] = jnp.zeros((8, ML_W), F32)
        c_ref[...] = jnp.zeros_like(c_ref)
        n_ref[...] = jnp.zeros_like(n_ref)
        m_ref[...] = jnp.zeros_like(m_ref)

    x = qk_ref[...].astype(F32)
    xx_ref[8:8 + CHUNK, :] = x
    conv = cb_ref[...]
    for j in range(ML_CONV):
        conv = conv + cw_ref[j:j + 1, :] * xx_ref[5 + j:5 + j + CHUNK, :]
    xx_ref[0:8, :] = x[CHUNK - 8:CHUNK, :]
    qk = conv * jax.nn.sigmoid(conv)

    gates = gt_ref[...] + gb_ref[...]
    ig = gates[:, :LANES]
    lf = jax.nn.log_sigmoid(gates[:, LANES:])
    row = lax.broadcasted_iota(I32, (CHUNK, CHUNK), 0)
    col = lax.broadcasted_iota(I32, (CHUNK, CHUNK), 1)
    causal = col <= row
    bcum = jnp.dot(causal.astype(F32), lf, precision=HIGHEST, preferred_element_type=F32)
    b_end = bcum[CHUNK - 1:CHUNK, :]
    m_st = m_ref[...]
    inter_all = bcum + m_st
    r_all = ig - bcum
    r_t = r_all.T
    w_log = b_end + r_all
    m_new = jnp.maximum(b_end + m_st, jnp.max(w_log, axis=0, keepdims=True))
    w_upd_all = jnp.exp(w_log - m_new)
    decay_all = jnp.exp(b_end + m_st - m_new)
    m_ref[...] = m_new

    lane = lax.broadcasted_iota(I32, (1, LANES), 1)
    sub = lax.broadcasted_iota(I32, (LANES, 1), 0)
    for h in range(ML_HEADS):
        j, half = divmod(h, 2)
        lo, hi = ML_DK * half, ML_DK * (half + 1)
        lane_sel = (lane >= lo) & (lane < hi)
        row_sel = (sub >= lo) & (sub < hi)
        hs = slice(h * LANES, (h + 1) * LANES)
        q_f = jnp.where(lane_sel, qk[:, j * LANES:(j + 1) * LANES], 0.0)
        k_f = jnp.where(lane_sel, qk[:, ML_HEADS * ML_DK + j * LANES:ML_HEADS * ML_DK + (j + 1) * LANES], 0.0) * (ML_DK ** -0.5)
        q_b = q_f.astype(BF16)
        b_col = bcum[:, h:h + 1]
        inter = inter_all[:, h:h + 1]
        d = jnp.where(causal, b_col + r_t[h:h + 1, :], -jnp.inf)
        m_t = jnp.maximum(inter, jnp.max(d, axis=-1, keepdims=True))
        w_intra = jnp.exp(d - m_t)
        w_inter = jnp.exp(inter - m_t)
        s = lax.dot_general(q_b, k_f.astype(BF16), (((1,), (1,)), ((), ())), preferred_element_type=F32) * w_intra
        v_h = v_ref[:, hs]
        c_pair = c_ref[j]
        n_pair = n_ref[j:j + 1, :]
        num = jnp.dot(s.astype(BF16), v_h, preferred_element_type=F32) + w_inter * jnp.dot(
            q_b, c_pair.astype(BF16), preferred_element_type=F32)
        den = jnp.sum(s, axis=-1, keepdims=True) + w_inter * jnp.sum(q_f * n_pair, axis=-1, keepdims=True)
        hh = num / jnp.maximum(jnp.abs(den), jnp.exp(-m_t))
        kw = k_f * w_upd_all[:, h:h + 1]
        decay = decay_all[:, h:h + 1]
        upd = lax.dot_general(kw.astype(BF16), v_h, (((0,), (0,)), ((), ())), preferred_element_type=F32)
        c_ref[j] = jnp.where(row_sel, decay * c_pair + upd, c_pair)
        n_ref[j:j + 1, :] = jnp.where(lane_sel, decay * n_pair + jnp.sum(kw, axis=0, keepdims=True), n_pair)
        hn = _rms(hh, ng_ref[:, hs])
        out_ref[:, hs] = (jax.nn.sigmoid(o_ref[:, hs].astype(F32)) * hn).astype(BF16)


def _mlstm(proj, gates, conv_w, conv_b, gate_b, norm_g, *, bsz, seq):
    n = proj.shape[0]
    nc = seq // CHUNK
    return pl.pallas_call(
        _mlstm_kernel,
        grid=(bsz, nc),
        in_specs=[
            pl.BlockSpec((CHUNK, ML_W), lambda b, c: (b * nc + c, COL_ML_QK // ML_W)),
            pl.BlockSpec((CHUNK, ML_W), lambda b, c: (b * nc + c, COL_ML_V // ML_W)),
            pl.BlockSpec((CHUNK, ML_W), lambda b, c: (b * nc + c, COL_ML_O // ML_W)),
            pl.BlockSpec((CHUNK, P_GATE), lambda b, c: (b * nc + c, 0)),
            pl.BlockSpec((ML_CONV, ML_W), lambda b, c: (0, 0)),
            pl.BlockSpec((1, ML_W), lambda b, c: (0, 0)),
            pl.BlockSpec((1, P_GATE), lambda b, c: (0, 0)),
            pl.BlockSpec((1, ML_W), lambda b, c: (0, 0)),
        ],
        out_specs=pl.BlockSpec((CHUNK, ML_W), lambda b, c: (b * nc + c, 0)),
        out_shape=jax.ShapeDtypeStruct((n, ML_W), BF16),
        scratch_shapes=[
            pltpu.VMEM((8 + CHUNK, ML_W), F32),
            pltpu.VMEM((ML_HEADS // 2, LANES, LANES), F32),
            pltpu.VMEM((8, LANES), F32),
            pltpu.VMEM((1, LANES), F32),
        ],
        compiler_params=_cparams(("parallel", "arbitrary")),
        name="mlstm",
    )(proj, proj, proj, gates, conv_w, conv_b, gate_b, norm_g)


def _qkprep_kernel(q_ref, k_ref, qg_ref, kg_ref, cos_ref, sa_ref, sb_ref, q0_ref, q1_ref, ko_ref):
    lane = lax.broadcasted_iota(I32, (1, LANES), 1)
    map0 = lane < DA_DK
    r = lax.broadcasted_iota(I32, (LANES, LANES), 0) // DA_DK
    c = lax.broadcasted_iota(I32, (LANES, LANES), 1) // DA_DK
    group_sum = (r == c).astype(F32)
    cos, sa, sb = cos_ref[...], sa_ref[...], sb_ref[...]

    def norm_rope(x, g):
        ss = jnp.dot(x * x, group_sum, precision=HIGHEST, preferred_element_type=F32)
        xn = x * lax.rsqrt(ss * (1.0 / DA_DK) + EPS) * g
        return xn * cos + pltpu.roll(xn, LANES - ROPE_DIM // 2, 1) * sa + pltpu.roll(xn, ROPE_DIM // 2, 1) * sb

    for h in range(DA_HEADS):
        hs = slice(h * LANES, (h + 1) * LANES)
        q = norm_rope(q_ref[:, hs].astype(F32), qg_ref[...]) * (DA_DK ** -0.5)
        q0_ref[:, hs] = jnp.where(map0, q, 0.0).astype(BF16)
        q1_ref[:, hs] = jnp.where(map0, 0.0, q).astype(BF16)
        ko_ref[:, hs] = norm_rope(k_ref[:, hs].astype(F32), kg_ref[...]).astype(BF16)


def _qkprep(proj, qg, kg, cos_t, sa_t, sb_t, *, bsz, seq, tt=512):
    n = proj.shape[0]
    ns = seq // tt
    tok = pl.BlockSpec((tt, DA_W), lambda b, s: (b * ns + s, 0))
    tab = pl.BlockSpec((tt, LANES), lambda b, s: (s, 0))
    vec = pl.BlockSpec((1, LANES), lambda b, s: (0, 0))
    return pl.pallas_call(
        _qkprep_kernel,
        grid=(bsz, ns),
        in_specs=[
            pl.BlockSpec((tt, DA_W), lambda b, s: (b * ns + s, COL_DA_Q // DA_W)),
            pl.BlockSpec((tt, DA_W), lambda b, s: (b * ns + s, COL_DA_K // DA_W)),
            vec, vec, tab, tab, tab,
        ],
        out_specs=[tok, tok, tok],
        out_shape=[jax.ShapeDtypeStruct((n, DA_W), BF16)] * 3,
        compiler_params=_cparams(("parallel", "parallel")),
        name="qkprep",
    )(proj, proj, qg, kg, cos_t, sa_t, sb_t)


def _flash_kernel(q0_ref, q1_ref, k_ref, v_ref, lam_ref, sg_ref, o_ref, m_sc, l_sc, acc_sc, *, tq, tk, lambda_init):
    qi = pl.program_id(2)
    q2 = jnp.concatenate([q0_ref[...], q1_ref[...]], axis=0)
    m_sc[...] = jnp.full_like(m_sc, -jnp.inf)
    l_sc[...] = jnp.zeros_like(l_sc)
    acc_sc[...] = jnp.zeros_like(acc_sc)
    r1 = lax.broadcasted_iota(I32, (tq, 1), 0) + qi * tq
    rowpos = jnp.concatenate([r1, r1], axis=0)
    col0 = lax.broadcasted_iota(I32, (1, tk), 1)

    def body(j, carry):
        start = pl.multiple_of(j * tk, tk)
        kj = k_ref[pl.ds(start, tk), :]
        vj = v_ref[pl.ds(start, tk), :]
        s = lax.dot_general(q2, kj, (((1,), (1,)), ((), ())), preferred_element_type=F32)
        s = jnp.where(col0 + j * tk <= rowpos, s, -jnp.inf)
        m_prev = m_sc[...]
        m_new = jnp.maximum(m_prev, jnp.max(s, axis=-1, keepdims=True))
        alpha = jnp.exp(m_prev - m_new)
        p = jnp.exp(s - m_new)
        l_sc[...] = alpha * l_sc[...] + jnp.sum(p, axis=-1, keepdims=True)
        acc_sc[...] = alpha * acc_sc[...] + jnp.dot(p.astype(BF16), vj, preferred_element_type=F32)
        m_sc[...] = m_new
        return carry

    n_kv = (qi * tq + tq + tk - 1) // tk
    lax.fori_loop(0, n_kv, body, 0)

    o = acc_sc[...] / l_sc[...]
    lp = lam_ref[...]
    lam = (jnp.exp(jnp.sum(lp[0:1, :] * lp[1:2, :], axis=-1, keepdims=True))
           - jnp.exp(jnp.sum(lp[2:3, :] * lp[3:4, :], axis=-1, keepdims=True)) + lambda_init)
    diff = o[:tq, :] - lam * o[tq:, :]
    o_ref[...] = (_rms(diff, sg_ref[...]) * (1.0 - lambda_init)).astype(BF16)


def _flash(q0, q1, kf, proj, lam_p, subln_g, *, bsz, seq, lambda_init, tq=256, tk=512):
    n = q0.shape[0]
    nq = seq // tq
    qspec = pl.BlockSpec((tq, LANES), lambda b, h, i: (b * nq + i, h))
    return pl.pallas_call(
        functools.partial(_flash_kernel, tq=tq, tk=tk, lambda_init=lambda_init),
        grid=(bsz, DA_HEADS, nq),
        in_specs=[
            qspec, qspec,
            pl.BlockSpec((seq, LANES), lambda b, h, i: (b, h)),
            pl.BlockSpec((seq, LANES), lambda b, h, i: (b, COL_DA_V // LANES + h)),
            pl.BlockSpec((4, DA_DK), lambda b, h, i: (0, 0)),
            pl.BlockSpec((1, LANES), lambda b, h, i: (0, 0)),
        ],
        out_specs=pl.BlockSpec((tq, LANES), lambda b, h, i: (b * nq + i, h)),
        out_shape=jax.ShapeDtypeStruct((n, DA_W), BF16),
        scratch_shapes=[
            pltpu.VMEM((2 * tq, 1), F32),
            pltpu.VMEM((2 * tq, 1), F32),
            pltpu.VMEM((2 * tq, LANES), F32),
        ],
        compiler_params=_cparams(("parallel", "parallel", "arbitrary")),
        name="flash",
    )(q0, q1, kf, proj, lam_p, subln_g)


def _outproj_router_kernel(ygm_ref, yml_ref, yda_ref, x_ref, wo_gm_ref, wo_ml_ref, wo_da_ref, g2_ref, wr_ref, br_ref,
                           x1_ref, hp_ref, info_ref, cnt_ref, carry_ref):
    tm = x_ref.shape[0]

    @pl.when(pl.program_id(0) == 0)
    def _():
        carry_ref[...] = jnp.zeros_like(carry_ref)

    x1 = (x_ref[...]
          + jnp.dot(ygm_ref[...], wo_gm_ref[...], preferred_element_type=F32)
          + jnp.dot(yml_ref[...], wo_ml_ref[...], preferred_element_type=F32)
          + jnp.dot(yda_ref[...], wo_da_ref[...], preferred_element_type=F32))
    x1_ref[...] = x1
    h2 = _rms(x1, g2_ref[...])
    half = D_MODEL // 2
    hi = pltpu.bitcast(h2[:, :half].astype(BF16).astype(F32), U32)
    lo = pltpu.bitcast(h2[:, half:].astype(BF16).astype(F32), U32)
    hp_ref[...] = hi | (lo >> 16)

    logits = jnp.dot(h2, wr_ref[...], precision=HIGHEST, preferred_element_type=F32) + br_ref[...]
    lane = lax.broadcasted_iota(I32, (tm, LANES), 1)
    lane_f = lane.astype(F32)
    is_g = (lane >= N_EXPERTS) & (lane < N_EXPERTS + N_GROUPS)
    lg = jnp.where(is_g, logits, -jnp.inf)
    mg = jnp.max(lg, axis=-1, keepdims=True)
    g_lane = jnp.min(jnp.where(lg == mg, lane_f, float(LANES)), axis=-1, keepdims=True)
    pg_top = 1.0 / jnp.sum(jnp.exp(lg - mg), axis=-1, keepdims=True)
    g_idx = g_lane.astype(I32) - N_EXPERTS
    in_group = (lane < N_EXPERTS) & ((lane // EXP_PER_GROUP) == g_idx)
    le = jnp.where(in_group, logits, -jnp.inf)
    l1 = jnp.max(le, axis=-1, keepdims=True)
    e1 = jnp.min(jnp.where(le == l1, lane_f, float(LANES)), axis=-1, keepdims=True)
    hot1 = lane_f == e1
    le2 = jnp.where(hot1, -jnp.inf, le)
    l2 = jnp.max(le2, axis=-1, keepdims=True)
    e2 = jnp.min(jnp.where(le2 == l2, lane_f, float(LANES)), axis=-1, keepdims=True)
    hot2 = lane_f == e2
    e21 = jnp.exp(l2 - l1)
    w1 = pg_top / (1.0 + e21)
    w2 = pg_top * e21 / (1.0 + e21)

    both = (hot1 | hot2).astype(BF16)
    r = lax.broadcasted_iota(I32, (tm, tm), 0)
    c = lax.broadcasted_iota(I32, (tm, tm), 1)
    before = (c < r).astype(BF16)
    prefix = jnp.dot(before, both, preferred_element_type=F32) + carry_ref[0:1, :]
    rank1 = jnp.sum(jnp.where(hot1, prefix, 0.0), axis=-1, keepdims=True)
    rank2 = jnp.sum(jnp.where(hot2, prefix, 0.0), axis=-1, keepdims=True)
    total = carry_ref[0:1, :] + jnp.sum(both.astype(F32), axis=0, keepdims=True)
    carry_ref[0:1, :] = total
    cnt_ref[...] = jnp.broadcast_to(total, cnt_ref.shape)

    info = jnp.where(lane == 0, e1, 0.0)
    for idx, val in enumerate((e2, w1, w2, rank1, rank2), start=1):
        info = jnp.where(lane == idx, val, info)
    info_ref[...] = info


def _outproj_router(ygm, yml, yda, x, wo, g2, wr, br, *, tm=512):
    n = x.shape[0]
    row = lambda w: pl.BlockSpec((tm, w), lambda i: (i, 0))
    full = lambda a, b: pl.BlockSpec((a, b), lambda i: (0, 0))
    return pl.pallas_call(
        _outproj_router_kernel,
        grid=(n // tm,),
        in_specs=[
            row(GM_W), row(ML_W), row(DA_W), row(D_MODEL),
            pl.BlockSpec((GM_W, D_MODEL), lambda i: (0, 0)),
            pl.BlockSpec((ML_W, D_MODEL), lambda i: (0, 0)),
            pl.BlockSpec((DA_W, D_MODEL), lambda i: (0, 0)),
            full(1, D_MODEL), full(D_MODEL, LANES), full(1, LANES),
        ],
        out_specs=[row(D_MODEL), row(D_MODEL // 2), row(LANES), full(8, LANES)],
        out_shape=[
            jax.ShapeDtypeStruct((n, D_MODEL), F32),
            jax.ShapeDtypeStruct((n, D_MODEL // 2), U32),
            jax.ShapeDtypeStruct((n, LANES), F32),
            jax.ShapeDtypeStruct((8, LANES), F32),
        ],
        scratch_shapes=[pltpu.VMEM((8, LANES), F32)],
        compiler_params=_cparams(("arbitrary",)),
        name="outproj_router",
    )(ygm, yml, yda, x, wo[0], wo[1], wo[2], g2, wr, br)


def _row_copy(src, dst, sem):
    return pltpu.make_async_copy(src, dst, sem)


def _dispatch_kernel(pos_ref, h_ref, xs_ref, sem):
    tt = h_ref.shape[0]

    def issue(r, carry):
        for k in range(2):
            _row_copy(h_ref.at[pl.ds(r, 1)], xs_ref.at[pl.ds(pos_ref[0, k, r], 1)], sem).start()
        return carry

    def drain(r, carry):
        for k in range(2):
            _row_copy(h_ref.at[pl.ds(r, 1)], xs_ref.at[pl.ds(pos_ref[0, k, r], 1)], sem).wait()
        return carry

    lax.fori_loop(0, tt, issue, 0)
    lax.fori_loop(0, tt, drain, 0)


def _dispatch(pos3, hp, *, tt=256):
    n, w = hp.shape
    return pl.pallas_call(
        _dispatch_kernel,
        grid=(n // tt,),
        in_specs=[
            pl.BlockSpec((1, 2, tt), lambda i: (i, 0, 0), memory_space=pltpu.SMEM),
            pl.BlockSpec((tt, w), lambda i: (i, 0)),
        ],
        out_specs=pl.BlockSpec(memory_space=pl.ANY),
        out_shape=jax.ShapeDtypeStruct((2 * n, w), U32),
        scratch_shapes=[pltpu.SemaphoreType.DMA(())],
        compiler_params=_cparams(("arbitrary",)),
        name="dispatch",
    )(pos3, hp)


def _gmm_kernel(tile_ref, exp_ref, lo_ref, hi_ref, first_ref, newexp_ref,
                xs_ref, wg_ref, wu_ref, wd_ref, o_ref, wg_sc, wu_sc, wd_sc):
    i = pl.program_id(0)
    tm = xs_ref.shape[0]

    @pl.when(newexp_ref[i] == 1)
    def _():
        wg_sc[...] = wg_ref[0].astype(BF16)
        wu_sc[...] = wu_ref[0].astype(BF16)
        wd_sc[...] = wd_ref[0].astype(BF16)

    @pl.when(first_ref[i] == 1)
    def _():
        o_ref[...] = jnp.zeros_like(o_ref)

    lo, hi = lo_ref[i], hi_ref[i]

    @pl.when(hi > lo)
    def _():
        half = D_MODEL // 2
        packed = xs_ref[...]
        xa = pltpu.bitcast(packed & jnp.uint32(0xFFFF0000), F32).astype(BF16)
        xb = pltpu.bitcast(packed << 16, F32).astype(BF16)
        g = (jnp.dot(xa, wg_sc[0:half, :], preferred_element_type=F32)
             + jnp.dot(xb, wg_sc[half:, :], preferred_element_type=F32))
        u = (jnp.dot(xa, wu_sc[0:half, :], preferred_element_type=F32)
             + jnp.dot(xb, wu_sc[half:, :], preferred_element_type=F32))
        hdn = (g * jax.nn.sigmoid(g) * u).astype(BF16)
        y = jnp.dot(hdn, wd_sc[...], preferred_element_type=F32)
        rows = lax.broadcasted_iota(I32, (tm, 1), 0) + tile_ref[i] * tm
        o_ref[...] = jnp.where((rows >= lo) & (rows < hi), y, o_ref[...])


def _gmm(meta, xs, w_gate, w_up, w_down, *, tm):
    rows = xs.shape[0]
    n_items = meta[0].shape[0]
    wspec = lambda a, b: pl.BlockSpec((1, a, b), lambda i, tile, exp, *_: (exp[i], 0, 0))
    grid_spec = pltpu.PrefetchScalarGridSpec(
        num_scalar_prefetch=6,
        grid=(n_items,),
        in_specs=[
            pl.BlockSpec((tm, D_MODEL // 2), lambda i, tile, *_: (tile[i], 0)),
            wspec(D_MODEL, D_EXPERT), wspec(D_MODEL, D_EXPERT), wspec(D_EXPERT, D_MODEL),
        ],
        out_specs=pl.BlockSpec((tm, D_MODEL), lambda i, tile, *_: (tile[i], 0)),
        scratch_shapes=[
            pltpu.VMEM((D_MODEL, D_EXPERT), BF16),
            pltpu.VMEM((D_MODEL, D_EXPERT), BF16),
            pltpu.VMEM((D_EXPERT, D_MODEL), BF16),
        ],
    )
    return pl.pallas_call(
        _gmm_kernel,
        grid_spec=grid_spec,
        out_shape=jax.ShapeDtypeStruct((rows, D_MODEL), F32),
        compiler_params=_cparams(("arbitrary",)),
        name="gmm",
    )(*meta, xs, w_gate, w_up, w_down)


def _gmm_items(counts, *, rows, tm):
    n_items = rows // tm + N_EXPERTS - 1
    ends = jnp.cumsum(counts)
    starts = ends - counts
    first_tile = starts // tm
    n_tiles = jnp.where(counts > 0, (ends - 1) // tm - first_tile + 1, 0)
    item_end = jnp.cumsum(n_tiles)
    item_start = item_end - n_tiles
    total = item_end[-1]
    idx = jnp.arange(n_items, dtype=I32)
    valid = idx < total
    last = jnp.maximum(total - 1, 0)
    src = jnp.where(valid, idx, last)
    exp = jnp.minimum(jnp.searchsorted(item_end, src, side="right").astype(I32), N_EXPERTS - 1)
    tile = (first_tile[exp] + src - item_start[exp]).astype(I32)
    lo = jnp.where(valid, starts[exp], 0).astype(I32)
    hi = jnp.where(valid, ends[exp], 0).astype(I32)
    prev_tile = jnp.concatenate([jnp.full((1,), -1, I32), tile[:-1]])
    prev_exp = jnp.concatenate([jnp.full((1,), -1, I32), exp[:-1]])
    first = (valid & (tile != prev_tile)).astype(I32)
    newexp = (valid & (exp != prev_exp)).astype(I32)
    return tile, exp, lo, hi, first, newexp


def _combine_kernel(pos_ref, x_ref, info_ref, ys_ref, o_ref, gbuf, sem):
    tt = x_ref.shape[0]

    def issue(r, carry):
        for k in range(2):
            _row_copy(ys_ref.at[pl.ds(pos_ref[0, k, r], 1)], gbuf.at[k, pl.ds(r, 1)], sem).start()
        return carry

    def drain(r, carry):
        for k in range(2):
            _row_copy(ys_ref.at[pl.ds(pos_ref[0, k, r], 1)], gbuf.at[k, pl.ds(r, 1)], sem).wait()
        return carry

    lax.fori_loop(0, tt, issue, 0)
    lax.fori_loop(0, tt, drain, 0)
    o_ref[...] = x_ref[...] + info_ref[:, 2:3] * gbuf[0] + info_ref[:, 3:4] * gbuf[1]


def _combine(pos3, x1, info, ys, *, tt=256):
    n = x1.shape[0]
    return pl.pallas_call(
        _combine_kernel,
        grid=(n // tt,),
        in_specs=[
            pl.BlockSpec((1, 2, tt), lambda i: (i, 0, 0), memory_space=pltpu.SMEM),
            pl.BlockSpec((tt, D_MODEL), lambda i: (i, 0)),
            pl.BlockSpec((tt, LANES), lambda i: (i, 0)),
            pl.BlockSpec(memory_space=pl.ANY),
        ],
        out_specs=pl.BlockSpec((tt, D_MODEL), lambda i: (i, 0)),
        out_shape=jax.ShapeDtypeStruct((n, D_MODEL), F32),
        scratch_shapes=[pltpu.VMEM((2, tt, D_MODEL), F32), pltpu.SemaphoreType.DMA(())],
        compiler_params=_cparams(("arbitrary",)),
        name="combine",
    )(pos3, x1, info, ys)


def _split_w_in(w):
    sizes = (GM_W, GM_W, ML_HEADS * ML_DK, ML_HEADS * ML_DK, ML_W, ML_W, ML_HEADS, ML_HEADS,
             2 * DA_HEADS * DA_DK, 2 * DA_HEADS * DA_DK, DA_W)
    offs = [0]
    for s in sizes:
        offs.append(offs[-1] + s)
    gm_u, gm_v, ml_q, ml_k, ml_v, ml_o, ml_i, ml_f, da_q, da_k, da_v = (w[:, a:b] for a, b in zip(offs[:-1], offs[1:]))
    main = jnp.concatenate([da_q, da_k, da_v, ml_v, ml_o, ml_q, ml_k, gm_u, gm_v], axis=1).astype(BF16)
    pad = jnp.zeros((w.shape[0], LANES - ML_HEADS), w.dtype)
    gate = jnp.concatenate([ml_i, pad, ml_f, pad], axis=1).astype(BF16)
    return main, gate


def _rope_tables(seq):
    pos = jnp.arange(seq, dtype=F32)
    inv_freq = ROPE_THETA ** (-jnp.arange(0, ROPE_DIM, 2, dtype=F32) / ROPE_DIM)
    ang = pos[:, None] * inv_freq[None, :]
    cos, sin = jnp.cos(ang), jnp.sin(ang)
    half = ROPE_DIM // 2
    z = lambda w: jnp.zeros((seq, w), F32)
    rest = DA_DK - ROPE_DIM
    cos_t = jnp.concatenate([cos, cos, jnp.ones((seq, rest), F32)], axis=1)
    sa_t = jnp.concatenate([-sin, z(half), z(rest)], axis=1)
    sb_t = jnp.concatenate([z(half), sin, z(rest)], axis=1)
    tile2 = lambda t: jnp.concatenate([t, t], axis=1)
    return tile2(cos_t), tile2(sa_t), tile2(sb_t)


def kernel(x, norm1_g, w_in, gm_vnorm_g, gm_ws, gm_b, ml_conv_w, ml_conv_b, ml_ig_b, ml_fg_b, ml_norm_g, da_qnorm_g, da_knorm_g, da_lambda, da_subln_g, w_out, norm2_g, moe_w_rg, moe_b_rg, moe_w_re, moe_b_re, moe_w_gate, moe_w_up, moe_w_down):
    bsz, seq, d = x.shape
    n = bsz * seq
    depth = w_in.shape[0]
    gmm_tm = 256
    tt_rows = 256
    cos_t, sa_t, sb_t = _rope_tables(seq)
    xf = x.reshape(n, d)
    row = lambda v: v.reshape(1, -1).astype(F32)
    lane_pad = lambda v, w: jnp.concatenate([v, jnp.zeros((w - v.shape[0],), v.dtype)])
    for l in range(depth):
        lambda_init = 0.8 - 0.6 * math.exp(-0.3 * l)
        w_main, w_gate = _split_w_in(w_in[l])
        proj, gates = _inproj(xf, row(norm1_g[l]), w_main, w_gate)

        y_gm = _gmlp(proj, row(gm_vnorm_g[l]), gm_ws[l], gm_b[l].T)

        gate_b = jnp.concatenate([lane_pad(ml_ig_b[l], LANES), lane_pad(ml_fg_b[l], LANES)]).reshape(1, P_GATE)
        y_ml = _mlstm(proj, gates, ml_conv_w[l], row(ml_conv_b[l]), gate_b, row(ml_norm_g[l]), bsz=bsz, seq=seq)

        qg = jnp.tile(da_qnorm_g[l], 2).reshape(1, LANES)
        kg = jnp.tile(da_knorm_g[l], 2).reshape(1, LANES)
        q0, q1, kf = _qkprep(proj, qg, kg, cos_t, sa_t, sb_t, bsz=bsz, seq=seq, tt=min(512, seq))
        y_da = _flash(q0, q1, kf, proj, da_lambda[l], row(da_subln_g[l]), bsz=bsz, seq=seq, lambda_init=lambda_init,
                      tq=min(256, seq), tk=min(512, seq))

        wo = w_out[l].astype(BF16)
        wo_parts = (wo[:GM_W], wo[GM_W:GM_W + ML_W], wo[GM_W + ML_W:])
        wr = jnp.concatenate([moe_w_re[l], moe_w_rg[l], jnp.zeros((d, LANES - N_EXPERTS - N_GROUPS), F32)], axis=1)
        br = lane_pad(jnp.concatenate([moe_b_re[l], moe_b_rg[l]]), LANES).reshape(1, LANES)
        x1, hp, info, cnt = _outproj_router(y_gm, y_ml, y_da, xf, wo_parts, row(norm2_g[l]), wr, br)

        counts = cnt[0, :N_EXPERTS].astype(I32)
        offsets = jnp.cumsum(counts) - counts
        e12 = info[:, 0:2].astype(I32)
        pos = offsets[e12] + info[:, 4:6].astype(I32)
        pos3 = pos.reshape(n // tt_rows, tt_rows, 2).transpose(0, 2, 1)
        meta = _gmm_items(counts, rows=2 * n, tm=gmm_tm)

        xs = _dispatch(pos3, hp, tt=tt_rows)
        ys = _gmm(meta, xs, moe_w_gate[l], moe_w_up[l], moe_w_down[l], tm=gmm_tm)
        xf = _combine(pos3, x1, info, ys, tt=tt_rows)
    return xf.reshape(bsz, seq, d)
```

```python
import functools
import math

import jax
import jax.numpy as jnp
from jax import lax
from jax.experimental import pallas as pl
from jax.experimental.pallas import tpu as pltpu

F32 = jnp.float32
BF16 = jnp.bfloat16
I32 = jnp.int32
U32 = jnp.uint32
HIGHEST = lax.Precision.HIGHEST
LOG2_E = math.log2(math.e)

LANES = 128
D_MODEL = 2048
CHUNK = 128
GM_GROUPS = 4
GM_W = 512
ML_HEADS = 6
ML_DK = 64
ML_W = 768
ML_CONV = 4
DA_HEADS = 6
DA_DK = 64
DA_W = 768
ROPE_THETA = 500000.0
ROPE_DIM = 16
N_GROUPS = 4
EXP_PER_GROUP = 8
N_EXPERTS = 32
D_EXPERT = 512
EPS = 1e-6

COL_DA_Q, COL_DA_K, COL_DA_V = 0, 768, 1536
COL_ML_V, COL_ML_O, COL_ML_QK = 2304, 3072, 3840
COL_GM_U, COL_GM_V = 4608, 5120
P_MAIN = 5632
P_GATE = 256

VMEM_LIMIT = 56 * 1024 * 1024


def _cparams(sem):
    return pltpu.CompilerParams(dimension_semantics=sem, vmem_limit_bytes=VMEM_LIMIT)


def _gelu(x):
    return 0.5 * x * (1.0 + lax.erf(x * (1.0 / math.sqrt(2.0))))


def _rms(x, g):
    return x * lax.rsqrt(jnp.mean(x * x, axis=-1, keepdims=True) + EPS) * g


_W_IN_MOVES = (
    (3340, 2304, COL_DA_Q),
    (1792, 1536, COL_ML_V),
    (1024, 768, COL_ML_QK),
    (0, 1024, COL_GM_U),
)
_W_IN_IG, _W_IN_FG = 3328, 3334


def _wprep_kernel(w_ref, main_ref, gate_ref):
    for src, width, dst in _W_IN_MOVES:
        main_ref[:, dst:dst + width] = w_ref[:, src:src + width].astype(BF16)
    gate_ref[...] = jnp.zeros_like(gate_ref)
    gate_ref[:, 0:ML_HEADS] = w_ref[:, _W_IN_IG:_W_IN_IG + ML_HEADS].astype(BF16)
    gate_ref[:, LANES:LANES + ML_HEADS] = w_ref[:, _W_IN_FG:_W_IN_FG + ML_HEADS].astype(BF16)


def _wprep(w_in, *, tr=256):
    depth, d, d_in = w_in.shape
    return pl.pallas_call(
        _wprep_kernel,
        grid=(depth, d // tr),
        in_specs=[pl.BlockSpec((None, tr, d_in), lambda l, i: (l, i, 0))],
        out_specs=[
            pl.BlockSpec((None, tr, P_MAIN), lambda l, i: (l, i, 0)),
            pl.BlockSpec((None, tr, P_GATE), lambda l, i: (l, i, 0)),
        ],
        out_shape=[
            jax.ShapeDtypeStruct((depth, d, P_MAIN), BF16),
            jax.ShapeDtypeStruct((depth, d, P_GATE), BF16),
        ],
        compiler_params=_cparams(("parallel", "parallel")),
        name="wprep",
    )(w_in)


def _inproj_kernel(x_ref, g_ref, w_ref, wg_ref, o_ref, og_ref, h_ref):
    @pl.when(pl.program_id(1) == 0)
    def _():
        h = _rms(x_ref[...], g_ref[...]).astype(BF16)
        h_ref[...] = h
        og_ref[...] = jnp.dot(h, wg_ref[...], preferred_element_type=F32)

    o_ref[...] = jnp.dot(h_ref[...], w_ref[...], preferred_element_type=F32).astype(BF16)


def _inproj(x, g, w_main, w_gate, layer, *, tm=512, tn=2816):
    n = x.shape[0]
    return pl.pallas_call(
        _inproj_kernel,
        grid=(n // tm, P_MAIN // tn),
        in_specs=[
            pl.BlockSpec((tm, D_MODEL), lambda i, j: (i, 0)),
            pl.BlockSpec((1, D_MODEL), lambda i, j: (0, 0)),
            pl.BlockSpec((None, D_MODEL, tn), lambda i, j: (layer, 0, j)),
            pl.BlockSpec((None, D_MODEL, P_GATE), lambda i, j: (layer, 0, 0)),
        ],
        out_specs=[
            pl.BlockSpec((tm, tn), lambda i, j: (i, j)),
            pl.BlockSpec((tm, P_GATE), lambda i, j: (i, 0)),
        ],
        out_shape=[
            jax.ShapeDtypeStruct((n, P_MAIN), BF16),
            jax.ShapeDtypeStruct((n, P_GATE), F32),
        ],
        scratch_shapes=[pltpu.VMEM((tm, D_MODEL), BF16)],
        compiler_params=_cparams(("parallel", "arbitrary")),
        name="inproj",
    )(x, g, w_main, w_gate)


def _gmlp_kernel(u_ref, v_ref, vg_ref, ws_ref, bt_ref, o_ref):
    tt = u_ref.shape[0]
    row = lax.broadcasted_iota(I32, (CHUNK, CHUNK), 0)
    col = lax.broadcasted_iota(I32, (CHUNK, CHUNK), 1)
    causal = col <= row
    for g in range(GM_GROUPS):
        cs = slice(g * LANES, (g + 1) * LANES)
        wsg = jnp.where(causal, ws_ref[g], 0.0).astype(BF16)
        bcol = bt_ref[:, g:g + 1]
        for c in range(tt // CHUNK):
            rs = slice(c * CHUNK, (c + 1) * CHUNK)
            v = _gelu(v_ref[rs, cs].astype(F32))
            vn = _rms(v, vg_ref[:, cs]).astype(BF16)
            z = jnp.dot(wsg, vn, preferred_element_type=F32) + bcol
            u = _gelu(u_ref[rs, cs].astype(F32))
            o_ref[rs, cs] = (u * z).astype(BF16)


def _gmlp(proj, vnorm_g, ws, b_t, *, tt=512):
    n = proj.shape[0]
    return pl.pallas_call(
        _gmlp_kernel,
        grid=(n // tt,),
        in_specs=[
            pl.BlockSpec((tt, GM_W), lambda i: (i, COL_GM_U // GM_W)),
            pl.BlockSpec((tt, GM_W), lambda i: (i, COL_GM_V // GM_W)),
            pl.BlockSpec((1, GM_W), lambda i: (0, 0)),
            pl.BlockSpec((GM_GROUPS, CHUNK, CHUNK), lambda i: (0, 0, 0)),
            pl.BlockSpec((CHUNK, GM_GROUPS), lambda i: (0, 0)),
        ],
        out_specs=pl.BlockSpec((tt, GM_W), lambda i: (i, 0)),
        out_shape=jax.ShapeDtypeStruct((n, GM_W), BF16),
        compiler_params=_cparams(("parallel",)),
        name="gmlp",
    )(proj, proj, vnorm_g, ws, b_t)


def _mlstm_kernel(qk_ref, v_ref, o_ref, gt_ref, cw_ref, cb_ref, gb_ref, ng_ref, out_ref,
                  xx_ref, c_ref, n_ref, m_ref):
    @pl.when(pl.program_id(1) == 0)
    def _():
        xx_ref[0:8, :] = jnp.zeros((8, ML_W), F32)
        c_ref[...] = jnp.zeros_like(c_ref)
        n_ref[...] = jnp.zeros_like(n_ref)
        m_ref[...] = jnp.zeros_like(m_ref)

    x = qk_ref[...].astype(F32)
    xx_ref[8:8 + CHUNK, :] = x
    conv = cb_ref[...]
    for j in range(ML_CONV):
        conv = conv + cw_ref[j:j + 1, :] * xx_ref[5 + j:5 + j + CHUNK, :]
    xx_ref[0:8, :] = x[CHUNK - 8:CHUNK, :]
    qk = conv * jax.nn.sigmoid(conv)

    gates = gt_ref[...] + gb_ref[...]
    ig = gates[:, :LANES]
    lf = jax.nn.log_sigmoid(gates[:, LANES:])
    row = lax.broadcasted_iota(I32, (CHUNK, CHUNK), 0)
    col = lax.broadcasted_iota(I32, (CHUNK, CHUNK), 1)
    causal = col <= row
    bcum = jnp.dot(causal.astype(F32), lf, precision=HIGHEST, preferred_element_type=F32)
    b_end = bcum[CHUNK - 1:CHUNK, :]
    m_st = m_ref[...]
    inter_all = bcum + m_st
    r_all = ig - bcum
    r_t = r_all.T
    w_log = b_end + r_all
    m_new = jnp.maximum(b_end + m_st, jnp.max(w_log, axis=0, keepdims=True))
    w_upd_all = jnp.exp(w_log - m_new)
    decay_all = jnp.exp(b_end + m_st - m_new)
    m_ref[...] = m_new

    lane = lax.broadcasted_iota(I32, (1, LANES), 1)
    sub = lax.broadcasted_iota(I32, (LANES, 1), 0)
    for h in range(ML_HEADS):
        j, half = divmod(h, 2)
        lo, hi = ML_DK * half, ML_DK * (half + 1)
        lane_sel = (lane >= lo) & (lane < hi)
        row_sel = (sub >= lo) & (sub < hi)
        hs = slice(h * LANES, (h + 1) * LANES)
        q_f = jnp.where(lane_sel, qk[:, j * LANES:(j + 1) * LANES], 0.0)
        k_f = jnp.where(lane_sel, qk[:, ML_HEADS * ML_DK + j * LANES:ML_HEADS * ML_DK + (j + 1) * LANES], 0.0) * (ML_DK ** -0.5)
        q_b = q_f.astype(BF16)
        b_col = bcum[:, h:h + 1]
        inter = inter_all[:, h:h + 1]
        d = jnp.where(causal, b_col + r_t[h:h + 1, :], -jnp.inf)
        m_t = jnp.maximum(inter, jnp.max(d, axis=-1, keepdims=True))
        w_intra = jnp.exp(d - m_t)
        w_inter = jnp.exp(inter - m_t)
        s = lax.dot_general(q_b, k_f.astype(BF16), (((1,), (1,)), ((), ())), preferred_element_type=F32) * w_intra
        v_h = v_ref[:, hs]
        c_pair = c_ref[j]
        n_pair = n_ref[j:j + 1, :]
        num = jnp.dot(s.astype(BF16), v_h, preferred_element_type=F32) + w_inter * jnp.dot(
            q_b, c_pair.astype(BF16), preferred_element_type=F32)
        den = jnp.sum(s, axis=-1, keepdims=True) + w_inter * jnp.sum(q_f * n_pair, axis=-1, keepdims=True)
        hh = num / jnp.maximum(jnp.abs(den), jnp.exp(-m_t))
        kw = k_f * w_upd_all[:, h:h + 1]
        decay = decay_all[:, h:h + 1]
        upd = lax.dot_general(kw.astype(BF16), v_h, (((0,), (0,)), ((), ())), preferred_element_type=F32)
        c_ref[j] = jnp.where(row_sel, decay * c_pair + upd, c_pair)
        n_ref[j:j + 1, :] = jnp.where(lane_sel, decay * n_pair + jnp.sum(kw, axis=0, keepdims=True), n_pair)
        hn = _rms(hh, ng_ref[:, hs])
        out_ref[:, hs] = (jax.nn.sigmoid(o_ref[:, hs].astype(F32)) * hn).astype(BF16)


def _mlstm(proj, gates, conv_w, conv_b, gate_b, norm_g, *, bsz, seq):
    n = proj.shape[0]
    nc = seq // CHUNK
    return pl.pallas_call(
        _mlstm_kernel,
        grid=(bsz, nc),
        in_specs=[
            pl.BlockSpec((CHUNK, ML_W), lambda b, c: (b * nc + c, COL_ML_QK // ML_W)),
            pl.BlockSpec((CHUNK, ML_W), lambda b, c: (b * nc + c, COL_ML_V // ML_W)),
            pl.BlockSpec((CHUNK, ML_W), lambda b, c: (b * nc + c, COL_ML_O // ML_W)),
            pl.BlockSpec((CHUNK, P_GATE), lambda b, c: (b * nc + c, 0)),
            pl.BlockSpec((ML_CONV, ML_W), lambda b, c: (0, 0)),
            pl.BlockSpec((1, ML_W), lambda b, c: (0, 0)),
            pl.BlockSpec((1, P_GATE), lambda b, c: (0, 0)),
            pl.BlockSpec((1, ML_W), lambda b, c: (0, 0)),
        ],
        out_specs=pl.BlockSpec((CHUNK, ML_W), lambda b, c: (b * nc + c, 0)),
        out_shape=jax.ShapeDtypeStruct((n, ML_W), BF16),
        scratch_shapes=[
            pltpu.VMEM((8 + CHUNK, ML_W), F32),
            pltpu.VMEM((ML_HEADS // 2, LANES, LANES), F32),
            pltpu.VMEM((8, LANES), F32),
            pltpu.VMEM((1, LANES), F32),
        ],
        compiler_params=_cparams(("parallel", "arbitrary")),
        name="mlstm",
    )(proj, proj, proj, gates, conv_w, conv_b, gate_b, norm_g)


def _qkprep_kernel(q_ref, k_ref, v_ref, qg_ref, kg_ref, cos_ref, sa_ref, sb_ref, q0t_ref, q1t_ref, ko_ref, vt_ref, *, tk):
    tt = q_ref.shape[0]
    lane = lax.broadcasted_iota(I32, (1, LANES), 1)
    map0 = lane < DA_DK
    r = lax.broadcasted_iota(I32, (LANES, LANES), 0) // DA_DK
    c = lax.broadcasted_iota(I32, (LANES, LANES), 1) // DA_DK
    group_sum = (r == c).astype(F32)
    cos, sa, sb = cos_ref[...], sa_ref[...], sb_ref[...]

    def norm_rope(x, g):
        ss = jnp.dot(x * x, group_sum, precision=HIGHEST, preferred_element_type=F32)
        xn = x * lax.rsqrt(ss * (1.0 / DA_DK) + EPS) * g
        return xn * cos + pltpu.roll(xn, LANES - ROPE_DIM // 2, 1) * sa + pltpu.roll(xn, ROPE_DIM // 2, 1) * sb

    for h in range(DA_HEADS):
        hs = slice(h * LANES, (h + 1) * LANES)
        q = norm_rope(q_ref[:, hs].astype(F32), qg_ref[...]) * (DA_DK ** -0.5 * LOG2_E)
        q0t_ref[hs, :] = jnp.where(map0, q, 0.0).T.astype(BF16)
        q1t_ref[hs, :] = jnp.where(map0, 0.0, q).T.astype(BF16)
        ko_ref[:, hs] = norm_rope(k_ref[:, hs].astype(F32), kg_ref[...]).astype(BF16)
        vt = v_ref[:, hs].astype(F32).T.astype(BF16)
        for j in range(tt // tk):
            vt_ref[h, j] = vt[:, j * tk:(j + 1) * tk]


def _qkprep(proj, qg, kg, cos_t, sa_t, sb_t, *, bsz, seq, tt, tk):
    n = proj.shape[0]
    ns = seq // tt
    tok_in = lambda col: pl.BlockSpec((tt, DA_W), lambda b, s: (b * ns + s, col // DA_W))
    qt = pl.BlockSpec((DA_W, tt), lambda b, s: (b, s))
    tab = pl.BlockSpec((tt, LANES), lambda b, s: (s, 0))
    vec = pl.BlockSpec((1, LANES), lambda b, s: (0, 0))
    return pl.pallas_call(
        functools.partial(_qkprep_kernel, tk=tk),
        grid=(bsz, ns),
        in_specs=[tok_in(COL_DA_Q), tok_in(COL_DA_K), tok_in(COL_DA_V), vec, vec, tab, tab, tab],
        out_specs=[
            qt, qt,
            pl.BlockSpec((tt, DA_W), lambda b, s: (b * ns + s, 0)),
            pl.BlockSpec((None, DA_HEADS, tt // tk, LANES, tk), lambda b, s: (b, 0, s, 0, 0)),
        ],
        out_shape=[
            jax.ShapeDtypeStruct((bsz * DA_W, seq), BF16),
            jax.ShapeDtypeStruct((bsz * DA_W, seq), BF16),
            jax.ShapeDtypeStruct((n, DA_W), BF16),
            jax.ShapeDtypeStruct((bsz, DA_HEADS, seq // tk, LANES, tk), BF16),
        ],
        compiler_params=_cparams(("parallel", "parallel")),
        name="qkprep",
    )(proj, proj, proj, qg, kg, cos_t, sa_t, sb_t)


def _flash_kernel(q0t_ref, q1t_ref, k_ref, vt_ref, lam_ref, sg_ref, o_ref, m_sc, l_sc, a_sc, acc_sc, p_sc,
                  *, tq, tk, lambda_init):
    qi = pl.program_id(2)
    qt = jnp.concatenate([q0t_ref[...], q1t_ref[...]], axis=1)
    m_sc[...] = jnp.full_like(m_sc, -jnp.inf)
    l_sc[...] = jnp.zeros_like(l_sc)
    a_sc[...] = jnp.ones_like(a_sc)
    acc_sc[...] = jnp.zeros_like(acc_sc)
    p_sc[...] = jnp.zeros_like(p_sc)

    def scores(j):
        start = pl.multiple_of(j * tk, tk)
        return jnp.dot(k_ref[pl.ds(start, tk), :], qt, preferred_element_type=F32)

    def flush(j_prev):
        acc_sc[...] = a_sc[...] * acc_sc[...] + jnp.dot(vt_ref[j_prev], p_sc[...], preferred_element_type=F32)

    def softmax(s, masked):
        if masked:
            kpos = lax.broadcasted_iota(I32, (tk, 1), 0)
            q1 = lax.broadcasted_iota(I32, (1, tq), 1)
            s = jnp.where(kpos <= jnp.concatenate([q1, q1], axis=1), s, -jnp.inf)
        m_prev = m_sc[...]
        m_new = jnp.maximum(m_prev, jnp.max(s, axis=0, keepdims=True))
        alpha = jnp.exp2(m_prev - m_new)
        p = jnp.exp2(s - m_new)
        l_sc[...] = alpha * l_sc[...] + jnp.sum(p, axis=0, keepdims=True)
        m_sc[...] = m_new
        return p.astype(BF16), alpha

    def body(j, s_cur):
        s_next = scores(j + 1)
        flush(jnp.maximum(j - 1, 0))
        p, alpha = softmax(s_cur, False)
        p_sc[...] = p
        a_sc[...] = alpha
        return s_next

    s_last = lax.fori_loop(0, qi, body, scores(0))
    flush(jnp.maximum(qi - 1, 0))
    p, alpha = softmax(s_last, True)
    acc = alpha * acc_sc[...] + jnp.dot(vt_ref[qi], p, preferred_element_type=F32)

    o = acc / l_sc[...]
    lp = lam_ref[...]
    lam = (jnp.exp(jnp.sum(lp[0:1, :] * lp[1:2, :], axis=-1, keepdims=True))
           - jnp.exp(jnp.sum(lp[2:3, :] * lp[3:4, :], axis=-1, keepdims=True)) + lambda_init)
    diff = o[:, :tq] - lam * o[:, tq:]
    y = diff * lax.rsqrt(jnp.mean(diff * diff, axis=0, keepdims=True) + EPS) * sg_ref[...] * (1.0 - lambda_init)
    o_ref[...] = y.T.astype(BF16)


def _flash(q0t, q1t, kf, vt, lam_p, subln_col, *, bsz, seq, lambda_init, tq):
    tk = tq
    n = bsz * seq
    nq = seq // tq
    qspec = pl.BlockSpec((LANES, tq), lambda b, h, i: (b * DA_HEADS + h, i))
    return pl.pallas_call(
        functools.partial(_flash_kernel, tq=tq, tk=tk, lambda_init=lambda_init),
        grid=(bsz, DA_HEADS, nq),
        in_specs=[
            qspec, qspec,
            pl.BlockSpec((seq, LANES), lambda b, h, i: (b, h)),
            pl.BlockSpec((None, None, seq // tk, LANES, tk), lambda b, h, i: (b, h, 0, 0, 0)),
            pl.BlockSpec((4, DA_DK), lambda b, h, i: (0, 0)),
            pl.BlockSpec((LANES, 1), lambda b, h, i: (0, 0)),
        ],
        out_specs=pl.BlockSpec((tq, LANES), lambda b, h, i: (b * nq + i, h)),
        out_shape=jax.ShapeDtypeStruct((n, DA_W), BF16),
        scratch_shapes=[
            pltpu.VMEM((1, 2 * tq), F32),
            pltpu.VMEM((1, 2 * tq), F32),
            pltpu.VMEM((1, 2 * tq), F32),
            pltpu.VMEM((LANES, 2 * tq), F32),
            pltpu.VMEM((tk, 2 * tq), BF16),
        ],
        compiler_params=_cparams(("parallel", "parallel", "arbitrary")),
        name="flash",
    )(q0t, q1t, kf, vt, lam_p, subln_col)


def _outproj_router_kernel(ygm_ref, yml_ref, yda_ref, x_ref, wo_gm_ref, wo_ml_ref, wo_da_ref, g2_ref, wr_ref, br_ref,
                           x1_ref, hp_ref, info_ref, cnt_ref, carry_ref):
    tm = x_ref.shape[0]

    @pl.when(pl.program_id(0) == 0)
    def _():
        carry_ref[...] = jnp.zeros_like(carry_ref)

    x1 = (x_ref[...]
          + jnp.dot(ygm_ref[...], wo_gm_ref[...], preferred_element_type=F32)
          + jnp.dot(yml_ref[...], wo_ml_ref[...], preferred_element_type=F32)
          + jnp.dot(yda_ref[...], wo_da_ref[...], preferred_element_type=F32))
    x1_ref[...] = x1
    h2 = _rms(x1, g2_ref[...])
    h2_hi = h2.astype(BF16)
    half = D_MODEL // 2
    hi = pltpu.bitcast(h2_hi[:, :half].astype(F32), U32)
    lo = pltpu.bitcast(h2_hi[:, half:].astype(F32), U32)
    hp_ref[...] = hi | (lo >> 16)

    h2_lo = (h2 - h2_hi.astype(F32)).astype(BF16)
    hw = jnp.dot(h2_hi, wr_ref[...], preferred_element_type=F32)
    logits = (hw[:, :LANES] + hw[:, LANES:]
              + jnp.dot(h2_lo, wr_ref[:, :LANES], preferred_element_type=F32) + br_ref[...])
    lane = lax.broadcasted_iota(I32, (tm, LANES), 1)
    lane_f = lane.astype(F32)
    is_g = (lane >= N_EXPERTS) & (lane < N_EXPERTS + N_GROUPS)
    lg = jnp.where(is_g, logits, -jnp.inf)
    mg = jnp.max(lg, axis=-1, keepdims=True)
    g_lane = jnp.min(jnp.where(lg == mg, lane_f, float(LANES)), axis=-1, keepdims=True)
    pg_top = 1.0 / jnp.sum(jnp.exp(lg - mg), axis=-1, keepdims=True)
    g_idx = g_lane.astype(I32) - N_EXPERTS
    in_group = (lane < N_EXPERTS) & (lax.shift_right_logical(lane, 3) == g_idx)
    le = jnp.where(in_group, logits, -jnp.inf)
    l1 = jnp.max(le, axis=-1, keepdims=True)
    e1 = jnp.min(jnp.where(le == l1, lane_f, float(LANES)), axis=-1, keepdims=True)
    hot1 = lane_f == e1
    le2 = jnp.where(hot1, -jnp.inf, le)
    l2 = jnp.max(le2, axis=-1, keepdims=True)
    e2 = jnp.min(jnp.where(le2 == l2, lane_f, float(LANES)), axis=-1, keepdims=True)
    hot2 = lane_f == e2
    e21 = jnp.exp(l2 - l1)
    w1 = pg_top / (1.0 + e21)
    w2 = pg_top * e21 / (1.0 + e21)

    both = (hot1 | hot2).astype(BF16)
    r = lax.broadcasted_iota(I32, (tm, tm), 0)
    c = lax.broadcasted_iota(I32, (tm, tm), 1)
    before = (c < r).astype(BF16)
    prefix = jnp.dot(before, both, preferred_element_type=F32) + carry_ref[0:1, :]
    rank1 = jnp.sum(jnp.where(hot1, prefix, 0.0), axis=-1, keepdims=True)
    rank2 = jnp.sum(jnp.where(hot2, prefix, 0.0), axis=-1, keepdims=True)
    total = carry_ref[0:1, :] + jnp.sum(both.astype(F32), axis=0, keepdims=True)
    carry_ref[0:1, :] = total
    cnt_ref[...] = jnp.broadcast_to(total, cnt_ref.shape)

    info = jnp.where(lane == 0, e1, 0.0)
    for idx, val in enumerate((e2, w1, w2, rank1, rank2), start=1):
        info = jnp.where(lane == idx, val, info)
    info_ref[...] = info


def _outproj_router(ygm, yml, yda, x, wo, g2, wr, br, *, tm=512):
    n = x.shape[0]
    row = lambda w: pl.BlockSpec((tm, w), lambda i: (i, 0))
    full = lambda a, b: pl.BlockSpec((a, b), lambda i: (0, 0))
    return pl.pallas_call(
        _outproj_router_kernel,
        grid=(n // tm,),
        in_specs=[
            row(GM_W), row(ML_W), row(DA_W), row(D_MODEL),
            pl.BlockSpec((GM_W, D_MODEL), lambda i: (0, 0)),
            pl.BlockSpec((ML_W, D_MODEL), lambda i: (0, 0)),
            pl.BlockSpec((DA_W, D_MODEL), lambda i: (0, 0)),
            full(1, D_MODEL), full(D_MODEL, 2 * LANES), full(1, LANES),
        ],
        out_specs=[row(D_MODEL), row(D_MODEL // 2), row(LANES), full(8, LANES)],
        out_shape=[
            jax.ShapeDtypeStruct((n, D_MODEL), F32),
            jax.ShapeDtypeStruct((n, D_MODEL // 2), U32),
            jax.ShapeDtypeStruct((n, LANES), F32),
            jax.ShapeDtypeStruct((8, LANES), F32),
        ],
        scratch_shapes=[pltpu.VMEM((8, LANES), F32)],
        compiler_params=_cparams(("arbitrary",)),
        name="outproj_router",
    )(ygm, yml, yda, x, wo[0], wo[1], wo[2], g2, wr, br)


ROW_DMA_UNROLL = 8


def _row_copy(src, dst, sem):
    return pltpu.make_async_copy(src, dst, sem)


def _dispatch_kernel(pos_ref, h_ref, xs_ref, sem):
    tt = h_ref.shape[0]

    def issue(r, carry):
        for k in range(2):
            _row_copy(h_ref.at[pl.ds(r, 1)], xs_ref.at[pl.ds(pos_ref[0, k, r], 1)], sem).start(priority=k)
        return carry

    def drain(r, carry):
        for k in range(2):
            _row_copy(h_ref.at[pl.ds(r, 1)], xs_ref.at[pl.ds(pos_ref[0, k, r], 1)], sem).wait()
        return carry

    lax.fori_loop(0, tt, issue, 0, unroll=ROW_DMA_UNROLL)
    lax.fori_loop(0, tt, drain, 0, unroll=ROW_DMA_UNROLL)


def _dispatch(pos3, hp, *, tt=256):
    n, w = hp.shape
    return pl.pallas_call(
        _dispatch_kernel,
        grid=(n // tt,),
        in_specs=[
            pl.BlockSpec((1, 2, tt), lambda i: (i, 0, 0), memory_space=pltpu.SMEM),
            pl.BlockSpec((tt, w), lambda i: (i, 0)),
        ],
        out_specs=pl.BlockSpec(memory_space=pl.ANY),
        out_shape=jax.ShapeDtypeStruct((2 * n, w), U32),
        scratch_shapes=[pltpu.SemaphoreType.DMA(())],
        compiler_params=_cparams(("arbitrary",)),
        name="dispatch",
    )(pos3, hp)


def _gmm_kernel(tile_ref, exp_ref, lo_ref, hi_ref, first_ref, newexp_ref,
                xs_ref, wg_ref, wu_ref, wd_ref, o_ref, wg_sc, wu_sc, wd_sc):
    i = pl.program_id(0)
    tm = xs_ref.shape[0]

    @pl.when(newexp_ref[i] == 1)
    def _():
        wg_sc[...] = wg_ref[...].astype(BF16)
        wu_sc[...] = wu_ref[...].astype(BF16)
        wd_sc[...] = wd_ref[...].astype(BF16)

    @pl.when(first_ref[i] == 1)
    def _():
        o_ref[...] = jnp.zeros_like(o_ref)

    lo, hi = lo_ref[i], hi_ref[i]

    @pl.when(hi > lo)
    def _():
        half = D_MODEL // 2
        packed = xs_ref[...]
        xa = pltpu.bitcast(packed & jnp.uint32(0xFFFF0000), F32).astype(BF16)
        xb = pltpu.bitcast(packed << 16, F32).astype(BF16)
        g = (jnp.dot(xa, wg_sc[0:half, :], preferred_element_type=F32)
             + jnp.dot(xb, wg_sc[half:, :], preferred_element_type=F32))
        u = (jnp.dot(xa, wu_sc[0:half, :], preferred_element_type=F32)
             + jnp.dot(xb, wu_sc[half:, :], preferred_element_type=F32))
        hdn = (g * jax.nn.sigmoid(g) * u).astype(BF16)
        y = jnp.dot(hdn, wd_sc[...], preferred_element_type=F32)
        rows = lax.broadcasted_iota(I32, (tm, 1), 0) + tile_ref[i] * tm
        o_ref[...] = jnp.where((rows >= lo) & (rows < hi), y, o_ref[...])


def _gmm(meta, xs, w_gate, w_up, w_down, layer, *, tm):
    rows = xs.shape[0]
    n_items = meta[0].shape[0]
    wspec = lambda a, b: pl.BlockSpec((None, None, a, b), lambda i, tile, exp, *_: (layer, exp[i], 0, 0))
    grid_spec = pltpu.PrefetchScalarGridSpec(
        num_scalar_prefetch=6,
        grid=(n_items,),
        in_specs=[
            pl.BlockSpec((tm, D_MODEL // 2), lambda i, tile, *_: (tile[i], 0)),
            wspec(D_MODEL, D_EXPERT), wspec(D_MODEL, D_EXPERT), wspec(D_EXPERT, D_MODEL),
        ],
        out_specs=pl.BlockSpec((tm, D_MODEL), lambda i, tile, *_: (tile[i], 0)),
        scratch_shapes=[
            pltpu.VMEM((D_MODEL, D_EXPERT), BF16),
            pltpu.VMEM((D_MODEL, D_EXPERT), BF16),
            pltpu.VMEM((D_EXPERT, D_MODEL), BF16),
        ],
    )
    return pl.pallas_call(
        _gmm_kernel,
        grid_spec=grid_spec,
        out_shape=jax.ShapeDtypeStruct((rows, D_MODEL), F32),
        compiler_params=_cparams(("arbitrary",)),
        name="gmm",
    )(*meta, xs, w_gate, w_up, w_down)


def _gmm_items(counts, *, rows, tm):
    n_items = rows // tm + N_EXPERTS - 1
    ends = jnp.cumsum(counts)
    starts = ends - counts
    first_tile = starts // tm
    n_tiles = jnp.where(counts > 0, (ends - 1) // tm - first_tile + 1, 0)
    item_end = jnp.cumsum(n_tiles)
    item_start = item_end - n_tiles
    total = item_end[-1]
    idx = jnp.arange(n_items, dtype=I32)
    valid = idx < total
    last = jnp.maximum(total - 1, 0)
    src = jnp.where(valid, idx, last)
    exp = jnp.minimum(jnp.sum((item_end[None, :] <= src[:, None]).astype(I32), axis=1), N_EXPERTS - 1)
    tile = (first_tile[exp] + src - item_start[exp]).astype(I32)
    lo = jnp.where(valid, starts[exp], 0).astype(I32)
    hi = jnp.where(valid, ends[exp], 0).astype(I32)
    prev_tile = jnp.concatenate([jnp.full((1,), -1, I32), tile[:-1]])
    prev_exp = jnp.concatenate([jnp.full((1,), -1, I32), exp[:-1]])
    first = (valid & (tile != prev_tile)).astype(I32)
    newexp = (valid & (exp != prev_exp)).astype(I32)
    return tile, exp, lo, hi, first, newexp


def _combine_kernel(pos_ref, x_ref, info_ref, ys_ref, o_ref, gbuf, sem):
    tt = x_ref.shape[0]

    def issue(r, carry):
        for k in range(2):
            _row_copy(ys_ref.at[pl.ds(pos_ref[0, k, r], 1)], gbuf.at[k, pl.ds(r, 1)], sem).start(priority=k)
        return carry

    def drain(r, carry):
        for k in range(2):
            _row_copy(ys_ref.at[pl.ds(pos_ref[0, k, r], 1)], gbuf.at[k, pl.ds(r, 1)], sem).wait()
        return carry

    lax.fori_loop(0, tt, issue, 0, unroll=ROW_DMA_UNROLL)
    lax.fori_loop(0, tt, drain, 0, unroll=ROW_DMA_UNROLL)
    o_ref[...] = x_ref[...] + info_ref[:, 2:3] * gbuf[0] + info_ref[:, 3:4] * gbuf[1]


def _combine(pos3, x1, info, ys, *, tt=256):
    n = x1.shape[0]
    return pl.pallas_call(
        _combine_kernel,
        grid=(n // tt,),
        in_specs=[
            pl.BlockSpec((1, 2, tt), lambda i: (i, 0, 0), memory_space=pltpu.SMEM),
            pl.BlockSpec((tt, D_MODEL), lambda i: (i, 0)),
            pl.BlockSpec((tt, LANES), lambda i: (i, 0)),
            pl.BlockSpec(memory_space=pl.ANY),
        ],
        out_specs=pl.BlockSpec((tt, D_MODEL), lambda i: (i, 0)),
        out_shape=jax.ShapeDtypeStruct((n, D_MODEL), F32),
        scratch_shapes=[pltpu.VMEM((2, tt, D_MODEL), F32), pltpu.SemaphoreType.DMA(())],
        compiler_params=_cparams(("arbitrary",)),
        name="combine",
    )(pos3, x1, info, ys)


def _rope_tables(seq):
    pos = jnp.arange(seq, dtype=F32)
    inv_freq = ROPE_THETA ** (-jnp.arange(0, ROPE_DIM, 2, dtype=F32) / ROPE_DIM)
    ang = pos[:, None] * inv_freq[None, :]
    cos, sin = jnp.cos(ang), jnp.sin(ang)
    half = ROPE_DIM // 2
    z = lambda w: jnp.zeros((seq, w), F32)
    rest = DA_DK - ROPE_DIM
    cos_t = jnp.concatenate([cos, cos, jnp.ones((seq, rest), F32)], axis=1)
    sa_t = jnp.concatenate([-sin, z(half), z(rest)], axis=1)
    sb_t = jnp.concatenate([z(half), sin, z(rest)], axis=1)
    tile2 = lambda t: jnp.concatenate([t, t], axis=1)
    return tile2(cos_t), tile2(sa_t), tile2(sb_t)


def kernel(x, norm1_g, w_in, gm_vnorm_g, gm_ws, gm_b, ml_conv_w, ml_conv_b, ml_ig_b, ml_fg_b, ml_norm_g, da_qnorm_g, da_knorm_g, da_lambda, da_subln_g, w_out, norm2_g, moe_w_rg, moe_b_rg, moe_w_re, moe_b_re, moe_w_gate, moe_w_up, moe_w_down):
    bsz, seq, d = x.shape
    n = bsz * seq
    depth = w_in.shape[0]
    gmm_tm = 256
    tt_rows = 256
    flash_tq = min(256, seq)
    cos_t, sa_t, sb_t = _rope_tables(seq)
    xf = x.reshape(n, d)
    row = lambda v: v.reshape(1, -1).astype(F32)
    lane_pad = lambda v, w: jnp.concatenate([v, jnp.zeros((w - v.shape[0],), v.dtype)])
    w_main, w_gate = _wprep(w_in)
    for l in range(depth):
        lambda_init = 0.8 - 0.6 * math.exp(-0.3 * l)
        proj, gates = _inproj(xf, row(norm1_g[l]), w_main, w_gate, l)

        y_gm = _gmlp(proj, row(gm_vnorm_g[l]), gm_ws[l], gm_b[l].T)

        gate_b = jnp.concatenate([lane_pad(ml_ig_b[l], LANES), lane_pad(ml_fg_b[l], LANES)]).reshape(1, P_GATE)
        y_ml = _mlstm(proj, gates, ml_conv_w[l], row(ml_conv_b[l]), gate_b, row(ml_norm_g[l]), bsz=bsz, seq=seq)

        qg = jnp.tile(da_qnorm_g[l], 2).reshape(1, LANES)
        kg = jnp.tile(da_knorm_g[l], 2).reshape(1, LANES)
        q0t, q1t, kf, vt = _qkprep(proj, qg, kg, cos_t, sa_t, sb_t, bsz=bsz, seq=seq, tt=min(512, seq), tk=flash_tq)
        y_da = _flash(q0t, q1t, kf, vt, da_lambda[l], da_subln_g[l].reshape(LANES, 1), bsz=bsz, seq=seq,
                      lambda_init=lambda_init, tq=flash_tq)

        wo = w_out[l].astype(BF16)
        wo_parts = (wo[:GM_W], wo[GM_W:GM_W + ML_W], wo[GM_W + ML_W:])
        wr = jnp.concatenate([moe_w_re[l], moe_w_rg[l], jnp.zeros((d, LANES - N_EXPERTS - N_GROUPS), F32)], axis=1)
        wr_hi = wr.astype(BF16)
        wr_lo = (wr - wr_hi.astype(F32)).astype(BF16)
        br = lane_pad(jnp.concatenate([moe_b_re[l], moe_b_rg[l]]), LANES).reshape(1, LANES)
        x1, hp, info, cnt = _outproj_router(y_gm, y_ml, y_da, xf, wo_parts, row(norm2_g[l]),
                                            jnp.concatenate([wr_hi, wr_lo], axis=1), br)

        counts = cnt[0, :N_EXPERTS].astype(I32)
        offsets = jnp.cumsum(counts) - counts
        e12 = info[:, 0:2].astype(I32)
        pos = offsets[e12] + info[:, 4:6].astype(I32)
        pos3 = pos.reshape(n // tt_rows, tt_rows, 2).transpose(0, 2, 1)
        meta = _gmm_items(counts, rows=2 * n, tm=gmm_tm)

        xs = _dispatch(pos3, hp, tt=tt_rows)
        ys = _gmm(meta, xs, moe_w_gate, moe_w_up, moe_w_down, l, tm=gmm_tm)
        xf = _combine(pos3, x1, info, ys, tt=tt_rows)
    return xf.reshape(bsz, seq, d)
```

```python
import functools
import math

import jax
import jax.numpy as jnp
from jax import lax
from jax.experimental import pallas as pl
from jax.experimental.pallas import tpu as pltpu

F32 = jnp.float32
BF16 = jnp.bfloat16
I32 = jnp.int32
U32 = jnp.uint32
HIGHEST = lax.Precision.HIGHEST
LOG2_E = math.log2(math.e)

LANES = 128
D_MODEL = 2048
CHUNK = 128
GM_GROUPS = 4
GM_W = 512
ML_HEADS = 6
ML_DK = 64
ML_W = 768
ML_CONV = 4
DA_HEADS = 6
DA_DK = 64
DA_W = 768
ROPE_THETA = 500000.0
ROPE_DIM = 16
N_GROUPS = 4
EXP_PER_GROUP = 8
N_EXPERTS = 32
D_EXPERT = 512
EPS = 1e-6

COL_DA_Q, COL_DA_K, COL_DA_V = 0, 768, 1536
COL_ML_V, COL_ML_O, COL_ML_QK = 2304, 3072, 3840
COL_GM_U, COL_GM_V = 4608, 5120
P_MAIN = 5632
P_GATE = 256

VMEM_LIMIT = 56 * 1024 * 1024


def _cparams(sem):
    return pltpu.CompilerParams(dimension_semantics=sem, vmem_limit_bytes=VMEM_LIMIT)


def _gelu(x):
    return 0.5 * x * (1.0 + lax.erf(x * (1.0 / math.sqrt(2.0))))


def _rms(x, g):
    return x * lax.rsqrt(jnp.mean(x * x, axis=-1, keepdims=True) + EPS) * g


_W_IN_MOVES = (
    (3340, 2304, COL_DA_Q),
    (1792, 1536, COL_ML_V),
    (1024, 768, COL_ML_QK),
    (0, 1024, COL_GM_U),
)
_W_IN_IG, _W_IN_FG = 3328, 3334


def _wprep_kernel(w_ref, main_ref, gate_ref):
    for src, width, dst in _W_IN_MOVES:
        main_ref[:, dst:dst + width] = w_ref[:, src:src + width].astype(BF16)
    gate_ref[...] = jnp.zeros_like(gate_ref)
    gate_ref[:, 0:ML_HEADS] = w_ref[:, _W_IN_IG:_W_IN_IG + ML_HEADS].astype(BF16)
    gate_ref[:, LANES:LANES + ML_HEADS] = w_ref[:, _W_IN_FG:_W_IN_FG + ML_HEADS].astype(BF16)


def _wprep(w_in, *, tr=256):
    depth, d, d_in = w_in.shape
    return pl.pallas_call(
        _wprep_kernel,
        grid=(depth, d // tr),
        in_specs=[pl.BlockSpec((None, tr, d_in), lambda l, i: (l, i, 0))],
        out_specs=[
            pl.BlockSpec((None, tr, P_MAIN), lambda l, i: (l, i, 0)),
            pl.BlockSpec((None, tr, P_GATE), lambda l, i: (l, i, 0)),
        ],
        out_shape=[
            jax.ShapeDtypeStruct((depth, d, P_MAIN), BF16),
            jax.ShapeDtypeStruct((depth, d, P_GATE), BF16),
        ],
        compiler_params=_cparams(("parallel", "parallel")),
        name="wprep",
    )(w_in)


def _inproj_kernel(x_ref, g_ref, w_ref, wg_ref, o_ref, og_ref, h_ref):
    @pl.when(pl.program_id(1) == 0)
    def _():
        h = _rms(x_ref[...], g_ref[...]).astype(BF16)
        h_ref[...] = h
        og_ref[...] = jnp.dot(h, wg_ref[...], preferred_element_type=F32)

    o_ref[...] = jnp.dot(h_ref[...], w_ref[...], preferred_element_type=F32).astype(BF16)


def _inproj(x, g, w_main, w_gate, layer, *, tm=512, tn=P_MAIN):
    n = x.shape[0]
    w_mode = dict(pipeline_mode=pl.Buffered(1)) if tn == P_MAIN else {}
    return pl.pallas_call(
        _inproj_kernel,
        grid=(n // tm, P_MAIN // tn),
        in_specs=[
            pl.BlockSpec((tm, D_MODEL), lambda i, j: (i, 0)),
            pl.BlockSpec((1, D_MODEL), lambda i, j: (0, 0)),
            pl.BlockSpec((None, D_MODEL, tn), lambda i, j: (layer, 0, j), **w_mode),
            pl.BlockSpec((None, D_MODEL, P_GATE), lambda i, j: (layer, 0, 0)),
        ],
        out_specs=[
            pl.BlockSpec((tm, tn), lambda i, j: (i, j)),
            pl.BlockSpec((tm, P_GATE), lambda i, j: (i, 0)),
        ],
        out_shape=[
            jax.ShapeDtypeStruct((n, P_MAIN), BF16),
            jax.ShapeDtypeStruct((n, P_GATE), F32),
        ],
        scratch_shapes=[pltpu.VMEM((tm, D_MODEL), BF16)],
        compiler_params=_cparams(("parallel", "arbitrary")),
        name="inproj",
    )(x, g, w_main, w_gate)


def _gmlp_kernel(u_ref, v_ref, vg_ref, ws_ref, bt_ref, o_ref):
    tt = u_ref.shape[0]
    row = lax.broadcasted_iota(I32, (CHUNK, CHUNK), 0)
    col = lax.broadcasted_iota(I32, (CHUNK, CHUNK), 1)
    causal = col <= row
    for g in range(GM_GROUPS):
        cs = slice(g * LANES, (g + 1) * LANES)
        wsg = jnp.where(causal, ws_ref[g], 0.0).astype(BF16)
        bcol = bt_ref[:, g:g + 1]
        for c in range(tt // CHUNK):
            rs = slice(c * CHUNK, (c + 1) * CHUNK)
            v = _gelu(v_ref[rs, cs].astype(F32))
            vn = _rms(v, vg_ref[:, cs]).astype(BF16)
            z = jnp.dot(wsg, vn, preferred_element_type=F32) + bcol
            u = _gelu(u_ref[rs, cs].astype(F32))
            o_ref[rs, cs] = (u * z).astype(BF16)


def _gmlp(proj, vnorm_g, ws, b_t, *, tt=512):
    n = proj.shape[0]
    return pl.pallas_call(
        _gmlp_kernel,
        grid=(n // tt,),
        in_specs=[
            pl.BlockSpec((tt, GM_W), lambda i: (i, COL_GM_U // GM_W)),
            pl.BlockSpec((tt, GM_W), lambda i: (i, COL_GM_V // GM_W)),
            pl.BlockSpec((1, GM_W), lambda i: (0, 0)),
            pl.BlockSpec((GM_GROUPS, CHUNK, CHUNK), lambda i: (0, 0, 0)),
            pl.BlockSpec((CHUNK, GM_GROUPS), lambda i: (0, 0)),
        ],
        out_specs=pl.BlockSpec((tt, GM_W), lambda i: (i, 0)),
        out_shape=jax.ShapeDtypeStruct((n, GM_W), BF16),
        compiler_params=_cparams(("parallel",)),
        name="gmlp",
    )(proj, proj, vnorm_g, ws, b_t)


def _mlstm_kernel(qk_ref, v_ref, o_ref, gt_ref, cw_ref, cb_ref, gb_ref, ng_ref, out_ref,
                  xx_ref, c_ref, n_ref, m_ref):
    @pl.when(pl.program_id(1) == 0)
    def _():
        xx_ref[0:8, :] = jnp.zeros((8, ML_W), F32)
        c_ref[...] = jnp.zeros_like(c_ref)
        n_ref[...] = jnp.zeros_like(n_ref)
        m_ref[...] = jnp.zeros_like(m_ref)

    x = qk_ref[...].astype(F32)
    xx_ref[8:8 + CHUNK, :] = x
    conv = cb_ref[...]
    for j in range(ML_CONV):
        conv = conv + cw_ref[j:j + 1, :] * xx_ref[5 + j:5 + j + CHUNK, :]
    xx_ref[0:8, :] = x[CHUNK - 8:CHUNK, :]
    qk = conv * jax.nn.sigmoid(conv)

    gates = gt_ref[...] + gb_ref[...]
    ig = gates[:, :LANES]
    lf = jax.nn.log_sigmoid(gates[:, LANES:])
    row = lax.broadcasted_iota(I32, (CHUNK, CHUNK), 0)
    col = lax.broadcasted_iota(I32, (CHUNK, CHUNK), 1)
    causal = col <= row
    bcum = jnp.dot(causal.astype(F32), lf, precision=HIGHEST, preferred_element_type=F32)
    b_end = bcum[CHUNK - 1:CHUNK, :]
    m_st = m_ref[...]
    inter_all = bcum + m_st
    r_all = ig - bcum
    r_t = r_all.T
    w_log = b_end + r_all
    m_new = jnp.maximum(b_end + m_st, jnp.max(w_log, axis=0, keepdims=True))
    w_upd_all = jnp.exp(w_log - m_new)
    decay_all = jnp.exp(b_end + m_st - m_new)
    m_ref[...] = m_new

    lane = lax.broadcasted_iota(I32, (1, LANES), 1)
    sub = lax.broadcasted_iota(I32, (LANES, 1), 0)
    for h in range(ML_HEADS):
        j, half = divmod(h, 2)
        lo, hi = ML_DK * half, ML_DK * (half + 1)
        lane_sel = (lane >= lo) & (lane < hi)
        row_sel = (sub >= lo) & (sub < hi)
        hs = slice(h * LANES, (h + 1) * LANES)
        q_f = jnp.where(lane_sel, qk[:, j * LANES:(j + 1) * LANES], 0.0)
        k_f = jnp.where(lane_sel, qk[:, ML_HEADS * ML_DK + j * LANES:ML_HEADS * ML_DK + (j + 1) * LANES], 0.0) * (ML_DK ** -0.5)
        q_b = q_f.astype(BF16)
        b_col = bcum[:, h:h + 1]
        inter = inter_all[:, h:h + 1]
        d = jnp.where(causal, b_col + r_t[h:h + 1, :], -jnp.inf)
        m_t = jnp.maximum(inter, jnp.max(d, axis=-1, keepdims=True))
        w_intra = jnp.exp(d - m_t)
        w_inter = jnp.exp(inter - m_t)
        s = lax.dot_general(q_b, k_f.astype(BF16), (((1,), (1,)), ((), ())), preferred_element_type=F32) * w_intra
        v_h = v_ref[:, hs]
        c_pair = c_ref[j]
        n_pair = n_ref[j:j + 1, :]
        num = jnp.dot(s.astype(BF16), v_h, preferred_element_type=F32) + w_inter * jnp.dot(
            q_b, c_pair.astype(BF16), preferred_element_type=F32)
        den = jnp.sum(s, axis=-1, keepdims=True) + w_inter * jnp.sum(q_f * n_pair, axis=-1, keepdims=True)
        hh = num / jnp.maximum(jnp.abs(den), jnp.exp(-m_t))
        kw = k_f * w_upd_all[:, h:h + 1]
        decay = decay_all[:, h:h + 1]
        upd = lax.dot_general(kw.astype(BF16), v_h, (((0,), (0,)), ((), ())), preferred_element_type=F32)
        c_ref[j] = jnp.where(row_sel, decay * c_pair + upd, c_pair)
        n_ref[j:j + 1, :] = jnp.where(lane_sel, decay * n_pair + jnp.sum(kw, axis=0, keepdims=True), n_pair)
        hn = _rms(hh, ng_ref[:, hs])
        out_ref[:, hs] = (jax.nn.sigmoid(o_ref[:, hs].astype(F32)) * hn).astype(BF16)


def _mlstm(proj, gates, conv_w, conv_b, gate_b, norm_g, *, bsz, seq):
    n = proj.shape[0]
    nc = seq // CHUNK
    return pl.pallas_call(
        _mlstm_kernel,
        grid=(bsz, nc),
        in_specs=[
            pl.BlockSpec((CHUNK, ML_W), lambda b, c: (b * nc + c, COL_ML_QK // ML_W)),
            pl.BlockSpec((CHUNK, ML_W), lambda b, c: (b * nc + c, COL_ML_V // ML_W)),
            pl.BlockSpec((CHUNK, ML_W), lambda b, c: (b * nc + c, COL_ML_O // ML_W)),
            pl.BlockSpec((CHUNK, P_GATE), lambda b, c: (b * nc + c, 0)),
            pl.BlockSpec((ML_CONV, ML_W), lambda b, c: (0, 0)),
            pl.BlockSpec((1, ML_W), lambda b, c: (0, 0)),
            pl.BlockSpec((1, P_GATE), lambda b, c: (0, 0)),
            pl.BlockSpec((1, ML_W), lambda b, c: (0, 0)),
        ],
        out_specs=pl.BlockSpec((CHUNK, ML_W), lambda b, c: (b * nc + c, 0)),
        out_shape=jax.ShapeDtypeStruct((n, ML_W), BF16),
        scratch_shapes=[
            pltpu.VMEM((8 + CHUNK, ML_W), F32),
            pltpu.VMEM((ML_HEADS // 2, LANES, LANES), F32),
            pltpu.VMEM((8, LANES), F32),
            pltpu.VMEM((1, LANES), F32),
        ],
        compiler_params=_cparams(("parallel", "arbitrary")),
        name="mlstm",
    )(proj, proj, proj, gates, conv_w, conv_b, gate_b, norm_g)


def _qkprep_kernel(q_ref, k_ref, v_ref, qg_ref, kg_ref, cos_ref, sa_ref, sb_ref, q0t_ref, q1t_ref, ko_ref, vt_ref, *, tk):
    tt = q_ref.shape[0]
    lane = lax.broadcasted_iota(I32, (1, LANES), 1)
    map0 = lane < DA_DK
    r = lax.broadcasted_iota(I32, (LANES, LANES), 0) // DA_DK
    c = lax.broadcasted_iota(I32, (LANES, LANES), 1) // DA_DK
    group_sum = (r == c).astype(F32)
    cos, sa, sb = cos_ref[...], sa_ref[...], sb_ref[...]

    def norm_rope(x, g):
        ss = jnp.dot(x * x, group_sum, precision=HIGHEST, preferred_element_type=F32)
        xn = x * lax.rsqrt(ss * (1.0 / DA_DK) + EPS) * g
        return xn * cos + pltpu.roll(xn, LANES - ROPE_DIM // 2, 1) * sa + pltpu.roll(xn, ROPE_DIM // 2, 1) * sb

    for h in range(DA_HEADS):
        hs = slice(h * LANES, (h + 1) * LANES)
        q = norm_rope(q_ref[:, hs].astype(F32), qg_ref[...]) * (DA_DK ** -0.5 * LOG2_E)
        q0t_ref[hs, :] = jnp.where(map0, q, 0.0).T.astype(BF16)
        q1t_ref[hs, :] = jnp.where(map0, 0.0, q).T.astype(BF16)
        ko_ref[:, hs] = norm_rope(k_ref[:, hs].astype(F32), kg_ref[...]).astype(BF16)
        vt = v_ref[:, hs].astype(F32).T.astype(BF16)
        for j in range(tt // tk):
            vt_ref[h, j] = vt[:, j * tk:(j + 1) * tk]


def _qkprep(proj, qg, kg, cos_t, sa_t, sb_t, *, bsz, seq, tt, tk):
    n = proj.shape[0]
    ns = seq // tt
    tok_in = lambda col: pl.BlockSpec((tt, DA_W), lambda b, s: (b * ns + s, col // DA_W))
    qt = pl.BlockSpec((DA_W, tt), lambda b, s: (b, s))
    tab = pl.BlockSpec((tt, LANES), lambda b, s: (s, 0))
    vec = pl.BlockSpec((1, LANES), lambda b, s: (0, 0))
    return pl.pallas_call(
        functools.partial(_qkprep_kernel, tk=tk),
        grid=(bsz, ns),
        in_specs=[tok_in(COL_DA_Q), tok_in(COL_DA_K), tok_in(COL_DA_V), vec, vec, tab, tab, tab],
        out_specs=[
            qt, qt,
            pl.BlockSpec((tt, DA_W), lambda b, s: (b * ns + s, 0)),
            pl.BlockSpec((None, DA_HEADS, tt // tk, LANES, tk), lambda b, s: (b, 0, s, 0, 0)),
        ],
        out_shape=[
            jax.ShapeDtypeStruct((bsz * DA_W, seq), BF16),
            jax.ShapeDtypeStruct((bsz * DA_W, seq), BF16),
            jax.ShapeDtypeStruct((n, DA_W), BF16),
            jax.ShapeDtypeStruct((bsz, DA_HEADS, seq // tk, LANES, tk), BF16),
        ],
        compiler_params=_cparams(("parallel", "parallel")),
        name="qkprep",
    )(proj, proj, proj, qg, kg, cos_t, sa_t, sb_t)


def _flash_kernel(q0t_ref, q1t_ref, k_ref, vt_ref, lam_ref, sg_ref, o_ref, m_sc, l_sc, a_sc, acc_sc, p_sc,
                  *, tq, lambda_init):
    tk = tq
    nq = k_ref.shape[0] // tq
    lp = lam_ref[...]
    lam = (jnp.exp(jnp.sum(lp[0:1, :] * lp[1:2, :], axis=-1, keepdims=True))
           - jnp.exp(jnp.sum(lp[2:3, :] * lp[3:4, :], axis=-1, keepdims=True)) + lambda_init)
    kpos = lax.broadcasted_iota(I32, (tk, 1), 0)
    q1 = lax.broadcasted_iota(I32, (1, tq), 1)
    causal = kpos <= jnp.concatenate([q1, q1], axis=1)

    def softmax(s, m_prev, l_prev):
        m_new = jnp.maximum(m_prev, jnp.max(s, axis=0, keepdims=True))
        alpha = jnp.exp2(m_prev - m_new)
        p = jnp.exp2(s - m_new)
        return p.astype(BF16), alpha, m_new, alpha * l_prev + jnp.sum(p, axis=0, keepdims=True)

    for qi in range(nq):
        qs = slice(qi * tq, (qi + 1) * tq)
        qt = jnp.concatenate([q0t_ref[:, qs], q1t_ref[:, qs]], axis=1)

        def scores(j, qt=qt):
            start = pl.multiple_of(j * tk, tk)
            return jnp.dot(k_ref[pl.ds(start, tk), :], qt, preferred_element_type=F32)

        if qi == 0:
            s_last = jnp.where(causal, scores(0), -jnp.inf)
            m_fin = jnp.max(s_last, axis=0, keepdims=True)
            p = jnp.exp2(s_last - m_fin)
            l_fin = jnp.sum(p, axis=0, keepdims=True)
            acc = jnp.dot(vt_ref[0], p.astype(BF16), preferred_element_type=F32)
        else:
            s0 = scores(0)
            m0 = jnp.max(s0, axis=0, keepdims=True)
            p0 = jnp.exp2(s0 - m0)
            m_sc[...] = m0
            l_sc[...] = jnp.sum(p0, axis=0, keepdims=True)
            p_sc[...] = p0.astype(BF16)
            a_sc[...] = jnp.ones_like(a_sc)
            acc_sc[...] = jnp.zeros_like(acc_sc)

            def body(j, s_cur, scores=scores):
                s_next = scores(j + 1)
                acc_sc[...] = a_sc[...] * acc_sc[...] + jnp.dot(vt_ref[j - 1], p_sc[...], preferred_element_type=F32)
                p, alpha, m_new, l_new = softmax(s_cur, m_sc[...], l_sc[...])
                p_sc[...] = p
                a_sc[...] = alpha
                m_sc[...] = m_new
                l_sc[...] = l_new
                return s_next

            s_last = lax.fori_loop(1, qi, body, scores(1))
            acc_prev = a_sc[...] * acc_sc[...] + jnp.dot(vt_ref[qi - 1], p_sc[...], preferred_element_type=F32)
            p, alpha, _, l_fin = softmax(jnp.where(causal, s_last, -jnp.inf), m_sc[...], l_sc[...])
            acc = alpha * acc_prev + jnp.dot(vt_ref[qi], p, preferred_element_type=F32)

        o = acc / l_fin
        diff = o[:, :tq] - lam * o[:, tq:]
        y = diff * lax.rsqrt(jnp.mean(diff * diff, axis=0, keepdims=True) + EPS) * sg_ref[...] * (1.0 - lambda_init)
        o_ref[qs, :] = y.T.astype(BF16)


def _flash(q0t, q1t, kf, vt, lam_p, subln_col, *, bsz, seq, lambda_init, tq):
    n = bsz * seq
    qspec = pl.BlockSpec((LANES, seq), lambda b, h: (b * DA_HEADS + h, 0))
    return pl.pallas_call(
        functools.partial(_flash_kernel, tq=tq, lambda_init=lambda_init),
        grid=(bsz, DA_HEADS),
        in_specs=[
            qspec, qspec,
            pl.BlockSpec((seq, LANES), lambda b, h: (b, h)),
            pl.BlockSpec((None, None, seq // tq, LANES, tq), lambda b, h: (b, h, 0, 0, 0)),
            pl.BlockSpec((4, DA_DK), lambda b, h: (0, 0)),
            pl.BlockSpec((LANES, 1), lambda b, h: (0, 0)),
        ],
        out_specs=pl.BlockSpec((seq, LANES), lambda b, h: (b, h)),
        out_shape=jax.ShapeDtypeStruct((n, DA_W), BF16),
        scratch_shapes=[
            pltpu.VMEM((1, 2 * tq), F32),
            pltpu.VMEM((1, 2 * tq), F32),
            pltpu.VMEM((1, 2 * tq), F32),
            pltpu.VMEM((LANES, 2 * tq), F32),
            pltpu.VMEM((tq, 2 * tq), BF16),
        ],
        compiler_params=_cparams(("parallel", "parallel")),
        name="flash",
    )(q0t, q1t, kf, vt, lam_p, subln_col)


def _outproj_router_kernel(ygm_ref, yml_ref, yda_ref, x_ref, wo_gm_ref, wo_ml_ref, wo_da_ref, g2_ref, wr_ref, br_ref,
                           x1_ref, hp_ref, info_ref, cnt_ref, carry_ref):
    tm = x_ref.shape[0]

    @pl.when(pl.program_id(0) == 0)
    def _():
        carry_ref[...] = jnp.zeros_like(carry_ref)

    x1 = (x_ref[...]
          + jnp.dot(ygm_ref[...], wo_gm_ref[...], preferred_element_type=F32)
          + jnp.dot(yml_ref[...], wo_ml_ref[...], preferred_element_type=F32)
          + jnp.dot(yda_ref[...], wo_da_ref[...], preferred_element_type=F32))
    x1_ref[...] = x1
    h2 = _rms(x1, g2_ref[...])
    h2_hi = h2.astype(BF16)
    half = D_MODEL // 2
    hi = pltpu.bitcast(h2_hi[:, :half].astype(F32), U32)
    lo = pltpu.bitcast(h2_hi[:, half:].astype(F32), U32)
    hp_ref[...] = hi | (lo >> 16)

    h2_lo = (h2 - h2_hi.astype(F32)).astype(BF16)
    hw = jnp.dot(h2_hi, wr_ref[...], preferred_element_type=F32)
    logits = (hw[:, :LANES] + hw[:, LANES:]
              + jnp.dot(h2_lo, wr_ref[:, :LANES], preferred_element_type=F32) + br_ref[...])
    lane = lax.broadcasted_iota(I32, (tm, LANES), 1)
    lane_f = lane.astype(F32)
    is_g = (lane >= N_EXPERTS) & (lane < N_EXPERTS + N_GROUPS)
    lg = jnp.where(is_g, logits, -jnp.inf)
    mg = jnp.max(lg, axis=-1, keepdims=True)
    g_lane = jnp.min(jnp.where(lg == mg, lane_f, float(LANES)), axis=-1, keepdims=True)
    pg_top = 1.0 / jnp.sum(jnp.exp(lg - mg), axis=-1, keepdims=True)
    g_idx = g_lane.astype(I32) - N_EXPERTS
    in_group = (lane < N_EXPERTS) & (lax.shift_right_logical(lane, 3) == g_idx)
    le = jnp.where(in_group, logits, -jnp.inf)
    l1 = jnp.max(le, axis=-1, keepdims=True)
    e1 = jnp.min(jnp.where(le == l1, lane_f, float(LANES)), axis=-1, keepdims=True)
    hot1 = lane_f == e1
    le2 = jnp.where(hot1, -jnp.inf, le)
    l2 = jnp.max(le2, axis=-1, keepdims=True)
    e2 = jnp.min(jnp.where(le2 == l2, lane_f, float(LANES)), axis=-1, keepdims=True)
    hot2 = lane_f == e2
    e21 = jnp.exp(l2 - l1)
    w1 = pg_top / (1.0 + e21)
    w2 = pg_top * e21 / (1.0 + e21)

    both = (hot1 | hot2).astype(BF16)
    r = lax.broadcasted_iota(I32, (tm, tm), 0)
    c = lax.broadcasted_iota(I32, (tm, tm), 1)
    before = (c < r).astype(BF16)
    prefix = jnp.dot(before, both, preferred_element_type=F32) + carry_ref[0:1, :]
    rank1 = jnp.sum(jnp.where(hot1, prefix, 0.0), axis=-1, keepdims=True)
    rank2 = jnp.sum(jnp.where(hot2, prefix, 0.0), axis=-1, keepdims=True)
    total = carry_ref[0:1, :] + jnp.sum(both.astype(F32), axis=0, keepdims=True)
    carry_ref[0:1, :] = total
    cnt_ref[...] = jnp.broadcast_to(total, cnt_ref.shape)

    info = jnp.where(lane == 0, e1, 0.0)
    for idx, val in enumerate((e2, w1, w2, rank1, rank2), start=1):
        info = jnp.where(lane == idx, val, info)
    info_ref[...] = info


def _outproj_router(ygm, yml, yda, x, wo, g2, wr, br, *, tm=512):
    n = x.shape[0]
    row = lambda w: pl.BlockSpec((tm, w), lambda i: (i, 0))
    full = lambda a, b: pl.BlockSpec((a, b), lambda i: (0, 0))
    return pl.pallas_call(
        _outproj_router_kernel,
        grid=(n // tm,),
        in_specs=[
            row(GM_W), row(ML_W), row(DA_W), row(D_MODEL),
            pl.BlockSpec((GM_W, D_MODEL), lambda i: (0, 0)),
            pl.BlockSpec((ML_W, D_MODEL), lambda i: (0, 0)),
            pl.BlockSpec((DA_W, D_MODEL), lambda i: (0, 0)),
            full(1, D_MODEL), full(D_MODEL, 2 * LANES), full(1, LANES),
        ],
        out_specs=[row(D_MODEL), row(D_MODEL // 2), row(LANES), full(8, LANES)],
        out_shape=[
            jax.ShapeDtypeStruct((n, D_MODEL), F32),
            jax.ShapeDtypeStruct((n, D_MODEL // 2), U32),
            jax.ShapeDtypeStruct((n, LANES), F32),
            jax.ShapeDtypeStruct((8, LANES), F32),
        ],
        scratch_shapes=[pltpu.VMEM((8, LANES), F32)],
        compiler_params=_cparams(("arbitrary",)),
        name="outproj_router",
    )(ygm, yml, yda, x, wo[0], wo[1], wo[2], g2, wr, br)


ROW_DMA_UNROLL = 8


def _row_dmas(tt, row_copy):
    def issue(r, carry):
        for k in range(2):
            row_copy(r, k).start(priority=k)
        return carry

    def drain(r, carry):
        for k in range(2):
            row_copy(r, k).wait()
        return carry

    lax.fori_loop(0, tt, issue, 0, unroll=ROW_DMA_UNROLL)
    lax.fori_loop(0, tt, drain, 0, unroll=ROW_DMA_UNROLL)


def _dispatch_kernel(pos0_ref, pos1_ref, h_ref, xs_ref, sem):
    pos = (pos0_ref, pos1_ref)
    _row_dmas(h_ref.shape[0],
              lambda r, k: pltpu.make_async_copy(h_ref.at[pl.ds(r, 1)], xs_ref.at[pl.ds(pos[k][r], 1)], sem))


def _dispatch(pos0, pos1, hp, *, tt=256):
    n, w = hp.shape
    idx = pl.BlockSpec((tt,), lambda i: (i,), memory_space=pltpu.SMEM)
    return pl.pallas_call(
        _dispatch_kernel,
        grid=(n // tt,),
        in_specs=[idx, idx, pl.BlockSpec((tt, w), lambda i: (i, 0))],
        out_specs=pl.BlockSpec(memory_space=pl.ANY),
        out_shape=jax.ShapeDtypeStruct((2 * n, w), U32),
        scratch_shapes=[pltpu.SemaphoreType.DMA(())],
        compiler_params=_cparams(("arbitrary",)),
        name="dispatch",
    )(pos0, pos1, hp)


def _gmm_kernel(tile_ref, exp_ref, lo_ref, hi_ref, first_ref, newexp_ref,
                xs_ref, wg_ref, wu_ref, wd_ref, o_ref, wg_sc, wu_sc, wd_sc):
    i = pl.program_id(0)
    tm = xs_ref.shape[0]

    @pl.when(newexp_ref[i] == 1)
    def _():
        wg_sc[...] = wg_ref[...].astype(BF16)
        wu_sc[...] = wu_ref[...].astype(BF16)
        wd_sc[...] = wd_ref[...].astype(BF16)

    @pl.when(first_ref[i] == 1)
    def _():
        o_ref[...] = jnp.zeros_like(o_ref)

    lo, hi = lo_ref[i], hi_ref[i]

    @pl.when(hi > lo)
    def _():
        half = D_MODEL // 2
        packed = xs_ref[...]
        xa = pltpu.bitcast(packed & jnp.uint32(0xFFFF0000), F32).astype(BF16)
        xb = pltpu.bitcast(packed << 16, F32).astype(BF16)
        g = (jnp.dot(xa, wg_sc[0:half, :], preferred_element_type=F32)
             + jnp.dot(xb, wg_sc[half:, :], preferred_element_type=F32))
        u = (jnp.dot(xa, wu_sc[0:half, :], preferred_element_type=F32)
             + jnp.dot(xb, wu_sc[half:, :], preferred_element_type=F32))
        hdn = (g * jax.nn.sigmoid(g) * u).astype(BF16)
        y = jnp.dot(hdn, wd_sc[...], preferred_element_type=F32)
        y_hi = pltpu.bitcast(y[:, :half].astype(BF16).astype(F32), U32)
        y_lo = pltpu.bitcast(y[:, half:].astype(BF16).astype(F32), U32)
        rows = lax.broadcasted_iota(I32, (tm, 1), 0) + tile_ref[i] * tm
        o_ref[...] = jnp.where((rows >= lo) & (rows < hi), y_hi | (y_lo >> 16), o_ref[...])


def _gmm(meta, xs, w_gate, w_up, w_down, layer, *, tm):
    rows = xs.shape[0]
    n_items = meta[0].shape[0]
    wspec = lambda a, b: pl.BlockSpec((None, None, a, b), lambda i, tile, exp, *_: (layer, exp[i], 0, 0))
    grid_spec = pltpu.PrefetchScalarGridSpec(
        num_scalar_prefetch=6,
        grid=(n_items,),
        in_specs=[
            pl.BlockSpec((tm, D_MODEL // 2), lambda i, tile, *_: (tile[i], 0)),
            wspec(D_MODEL, D_EXPERT), wspec(D_MODEL, D_EXPERT), wspec(D_EXPERT, D_MODEL),
        ],
        out_specs=pl.BlockSpec((tm, D_MODEL // 2), lambda i, tile, *_: (tile[i], 0)),
        scratch_shapes=[
            pltpu.VMEM((D_MODEL, D_EXPERT), BF16),
            pltpu.VMEM((D_MODEL, D_EXPERT), BF16),
            pltpu.VMEM((D_EXPERT, D_MODEL), BF16),
        ],
    )
    return pl.pallas_call(
        _gmm_kernel,
        grid_spec=grid_spec,
        out_shape=jax.ShapeDtypeStruct((rows, D_MODEL // 2), U32),
        compiler_params=_cparams(("arbitrary",)),
        name="gmm",
    )(*meta, xs, w_gate, w_up, w_down)


def _gmm_items(counts, *, rows, tm):
    n_items = rows // tm + N_EXPERTS - 1
    ends = jnp.cumsum(counts)
    starts = ends - counts
    first_tile = starts // tm
    n_tiles = jnp.where(counts > 0, (ends - 1) // tm - first_tile + 1, 0)
    item_end = jnp.cumsum(n_tiles)
    item_start = item_end - n_tiles
    total = item_end[-1]
    idx = jnp.arange(n_items, dtype=I32)
    valid = idx < total
    last = jnp.maximum(total - 1, 0)
    src = jnp.where(valid, idx, last)
    exp = jnp.minimum(jnp.sum((item_end[None, :] <= src[:, None]).astype(I32), axis=1), N_EXPERTS - 1)
    tile = (first_tile[exp] + src - item_start[exp]).astype(I32)
    lo = jnp.where(valid, starts[exp], 0).astype(I32)
    hi = jnp.where(valid, ends[exp], 0).astype(I32)
    prev_tile = jnp.concatenate([jnp.full((1,), -1, I32), tile[:-1]])
    prev_exp = jnp.concatenate([jnp.full((1,), -1, I32), exp[:-1]])
    first = (valid & (tile != prev_tile)).astype(I32)
    newexp = (valid & (exp != prev_exp)).astype(I32)
    return tile, exp, lo, hi, first, newexp


def _combine_kernel(pos0_ref, pos1_ref, x_ref, info_ref, ys_ref, o_ref, gbuf, sem):
    pos = (pos0_ref, pos1_ref)
    _row_dmas(x_ref.shape[0],
              lambda r, k: pltpu.make_async_copy(ys_ref.at[pl.ds(pos[k][r], 1)], gbuf.at[k, pl.ds(r, 1)], sem))
    half = D_MODEL // 2
    w = (info_ref[:, 2:3], info_ref[:, 3:4])
    upper = lambda k: pltpu.bitcast(gbuf[k] & jnp.uint32(0xFFFF0000), F32)
    lower = lambda k: pltpu.bitcast(gbuf[k] << 16, F32)
    o_ref[:, :half] = x_ref[:, :half] + w[0] * upper(0) + w[1] * upper(1)
    o_ref[:, half:] = x_ref[:, half:] + w[0] * lower(0) + w[1] * lower(1)


def _combine(pos0, pos1, x1, info, ys, *, tt=256):
    n = x1.shape[0]
    idx = pl.BlockSpec((tt,), lambda i: (i,), memory_space=pltpu.SMEM)
    return pl.pallas_call(
        _combine_kernel,
        grid=(n // tt,),
        in_specs=[
            idx, idx,
            pl.BlockSpec((tt, D_MODEL), lambda i: (i, 0)),
            pl.BlockSpec((tt, LANES), lambda i: (i, 0)),
            pl.BlockSpec(memory_space=pl.ANY),
        ],
        out_specs=pl.BlockSpec((tt, D_MODEL), lambda i: (i, 0)),
        out_shape=jax.ShapeDtypeStruct((n, D_MODEL), F32),
        scratch_shapes=[pltpu.VMEM((2, tt, D_MODEL // 2), U32), pltpu.SemaphoreType.DMA(())],
        compiler_params=_cparams(("arbitrary",)),
        name="combine",
    )(pos0, pos1, x1, info, ys)


def _rope_tables(seq):
    pos = jnp.arange(seq, dtype=F32)
    inv_freq = ROPE_THETA ** (-jnp.arange(0, ROPE_DIM, 2, dtype=F32) / ROPE_DIM)
    ang = pos[:, None] * inv_freq[None, :]
    cos, sin = jnp.cos(ang), jnp.sin(ang)
    half = ROPE_DIM // 2
    z = lambda w: jnp.zeros((seq, w), F32)
    rest = DA_DK - ROPE_DIM
    cos_t = jnp.concatenate([cos, cos, jnp.ones((seq, rest), F32)], axis=1)
    sa_t = jnp.concatenate([-sin, z(half), z(rest)], axis=1)
    sb_t = jnp.concatenate([z(half), sin, z(rest)], axis=1)
    tile2 = lambda t: jnp.concatenate([t, t], axis=1)
    return tile2(cos_t), tile2(sa_t), tile2(sb_t)


def kernel(x, norm1_g, w_in, gm_vnorm_g, gm_ws, gm_b, ml_conv_w, ml_conv_b, ml_ig_b, ml_fg_b, ml_norm_g, da_qnorm_g, da_knorm_g, da_lambda, da_subln_g, w_out, norm2_g, moe_w_rg, moe_b_rg, moe_w_re, moe_b_re, moe_w_gate, moe_w_up, moe_w_down):
    bsz, seq, d = x.shape
    n = bsz * seq
    depth = w_in.shape[0]
    gmm_tm = 512
    tt_rows = 256
    flash_tq = min(512, seq)
    cos_t, sa_t, sb_t = _rope_tables(seq)
    xf = x.reshape(n, d)
    row = lambda v: v.reshape(1, -1).astype(F32)
    lane_pad = lambda v, w: jnp.concatenate([v, jnp.zeros((w - v.shape[0],), v.dtype)])
    w_main, w_gate = _wprep(w_in)
    for l in range(depth):
        lambda_init = 0.8 - 0.6 * math.exp(-0.3 * l)
        proj, gates = _inproj(xf, row(norm1_g[l]), w_main, w_gate, l)

        y_gm = _gmlp(proj, row(gm_vnorm_g[l]), gm_ws[l], gm_b[l].T)

        gate_b = jnp.concatenate([lane_pad(ml_ig_b[l], LANES), lane_pad(ml_fg_b[l], LANES)]).reshape(1, P_GATE)
        y_ml = _mlstm(proj, gates, ml_conv_w[l], row(ml_conv_b[l]), gate_b, row(ml_norm_g[l]), bsz=bsz, seq=seq)

        qg = jnp.tile(da_qnorm_g[l], 2).reshape(1, LANES)
        kg = jnp.tile(da_knorm_g[l], 2).reshape(1, LANES)
        q0t, q1t, kf, vt = _qkprep(proj, qg, kg, cos_t, sa_t, sb_t, bsz=bsz, seq=seq, tt=min(512, seq), tk=flash_tq)
        y_da = _flash(q0t, q1t, kf, vt, da_lambda[l], da_subln_g[l].reshape(LANES, 1), bsz=bsz, seq=seq,
                      lambda_init=lambda_init, tq=flash_tq)

        wo = w_out[l].astype(BF16)
        wo_parts = (wo[:GM_W], wo[GM_W:GM_W + ML_W], wo[GM_W + ML_W:])
        wr = jnp.concatenate([moe_w_re[l], moe_w_rg[l], jnp.zeros((d, LANES - N_EXPERTS - N_GROUPS), F32)], axis=1)
        wr_hi = wr.astype(BF16)
        wr_lo = (wr - wr_hi.astype(F32)).astype(BF16)
        br = lane_pad(jnp.concatenate([moe_b_re[l], moe_b_rg[l]]), LANES).reshape(1, LANES)
        x1, hp, info, cnt = _outproj_router(y_gm, y_ml, y_da, xf, wo_parts, row(norm2_g[l]),
                                            jnp.concatenate([wr_hi, wr_lo], axis=1), br)

        counts = cnt[0, :N_EXPERTS].astype(I32)
        offsets = jnp.cumsum(counts) - counts
        e12 = info[:, 0:2].astype(I32)
        pos = offsets[e12] + info[:, 4:6].astype(I32)
        pos0, pos1 = pos[:, 0], pos[:, 1]
        meta = _gmm_items(counts, rows=2 * n, tm=gmm_tm)

        xs = _dispatch(pos0, pos1, hp, tt=tt_rows)
        ys = _gmm(meta, xs, moe_w_gate, moe_w_up, moe_w_down, l, tm=gmm_tm)
        xf = _combine(pos0, pos1, x1, info, ys, tt=tt_rows)
    return xf.reshape(bsz, seq, d)
```

```python
import functools
import math

import jax
import jax.numpy as jnp
from jax import lax
from jax.experimental import pallas as pl
from jax.experimental.pallas import tpu as pltpu

F32 = jnp.float32
BF16 = jnp.bfloat16
I32 = jnp.int32
U32 = jnp.uint32
HIGHEST = lax.Precision.HIGHEST
LOG2_E = math.log2(math.e)

LANES = 128
D_MODEL = 2048
CHUNK = 128
GM_GROUPS = 4
GM_W = 512
ML_HEADS = 6
ML_DK = 64
ML_W = 768
ML_CONV = 4
DA_HEADS = 6
DA_DK = 64
DA_W = 768
ROPE_THETA = 500000.0
ROPE_DIM = 16
N_GROUPS = 4
EXP_PER_GROUP = 8
N_EXPERTS = 32
D_EXPERT = 512
EPS = 1e-6

COL_DA_Q, COL_DA_K, COL_DA_V = 0, 768, 1536
COL_ML_V, COL_ML_O, COL_ML_QK = 2304, 3072, 3840
COL_GM_U, COL_GM_V = 4608, 5120
P_MAIN = 5632
P_GATE = 256

VMEM_LIMIT = 56 * 1024 * 1024


def _cparams(sem):
    return pltpu.CompilerParams(dimension_semantics=sem, vmem_limit_bytes=VMEM_LIMIT)


def _gelu(x):
    return 0.5 * x * (1.0 + lax.erf(x * (1.0 / math.sqrt(2.0))))


def _rms(x, g):
    return x * lax.rsqrt(jnp.mean(x * x, axis=-1, keepdims=True) + EPS) * g


_W_IN_MOVES = (
    (3340, 2304, COL_DA_Q),
    (1792, 1536, COL_ML_V),
    (1024, 768, COL_ML_QK),
    (0, 1024, COL_GM_U),
)
_W_IN_IG, _W_IN_FG = 3328, 3334


def _wprep_kernel(w_ref, main_ref, gate_ref):
    for src, width, dst in _W_IN_MOVES:
        main_ref[:, dst:dst + width] = w_ref[:, src:src + width].astype(BF16)
    gate_ref[...] = jnp.zeros_like(gate_ref)
    gate_ref[:, 0:ML_HEADS] = w_ref[:, _W_IN_IG:_W_IN_IG + ML_HEADS].astype(BF16)
    gate_ref[:, LANES:LANES + ML_HEADS] = w_ref[:, _W_IN_FG:_W_IN_FG + ML_HEADS].astype(BF16)


def _wprep(w_in, *, tr=256):
    depth, d, d_in = w_in.shape
    return pl.pallas_call(
        _wprep_kernel,
        grid=(depth, d // tr),
        in_specs=[pl.BlockSpec((None, tr, d_in), lambda l, i: (l, i, 0))],
        out_specs=[
            pl.BlockSpec((None, tr, P_MAIN), lambda l, i: (l, i, 0)),
            pl.BlockSpec((None, tr, P_GATE), lambda l, i: (l, i, 0)),
        ],
        out_shape=[
            jax.ShapeDtypeStruct((depth, d, P_MAIN), BF16),
            jax.ShapeDtypeStruct((depth, d, P_GATE), BF16),
        ],
        compiler_params=_cparams(("parallel", "parallel")),
        name="wprep",
    )(w_in)


def _inproj_kernel(x_ref, g_ref, w_ref, wg_ref, o_ref, og_ref):
    h = _rms(x_ref[...], g_ref[...]).astype(BF16)
    og_ref[...] = jnp.dot(h, wg_ref[...], preferred_element_type=F32)
    o_ref[...] = jnp.dot(h, w_ref[...], preferred_element_type=F32).astype(BF16)


def _inproj(x, g, w_main, w_gate, layer, *, tm=512):
    n = x.shape[0]
    resident = dict(pipeline_mode=pl.Buffered(1))
    return pl.pallas_call(
        _inproj_kernel,
        grid=(n // tm,),
        in_specs=[
            pl.BlockSpec((tm, D_MODEL), lambda i: (i, 0)),
            pl.BlockSpec((1, D_MODEL), lambda i: (0, 0)),
            pl.BlockSpec((None, D_MODEL, P_MAIN), lambda i: (layer, 0, 0), **resident),
            pl.BlockSpec((None, D_MODEL, P_GATE), lambda i: (layer, 0, 0), **resident),
        ],
        out_specs=[
            pl.BlockSpec((tm, P_MAIN), lambda i: (i, 0)),
            pl.BlockSpec((tm, P_GATE), lambda i: (i, 0)),
        ],
        out_shape=[
            jax.ShapeDtypeStruct((n, P_MAIN), BF16),
            jax.ShapeDtypeStruct((n, P_GATE), F32),
        ],
        compiler_params=_cparams(("parallel",)),
        name="inproj",
    )(x, g, w_main, w_gate)


def _gmlp_kernel(u_ref, v_ref, vg_ref, ws_ref, bt_ref, o_ref):
    tt = u_ref.shape[0]
    row = lax.broadcasted_iota(I32, (CHUNK, CHUNK), 0)
    col = lax.broadcasted_iota(I32, (CHUNK, CHUNK), 1)
    causal = col <= row
    for g in range(GM_GROUPS):
        cs = slice(g * LANES, (g + 1) * LANES)
        wsg = jnp.where(causal, ws_ref[g], 0.0).astype(BF16)
        bcol = bt_ref[:, g:g + 1]
        for c in range(tt // CHUNK):
            rs = slice(c * CHUNK, (c + 1) * CHUNK)
            v = _gelu(v_ref[rs, cs].astype(F32))
            vn = _rms(v, vg_ref[:, cs]).astype(BF16)
            z = jnp.dot(wsg, vn, preferred_element_type=F32) + bcol
            u = _gelu(u_ref[rs, cs].astype(F32))
            o_ref[rs, cs] = (u * z).astype(BF16)


def _gmlp(proj, vnorm_g, ws, b_t, *, tt=512):
    n = proj.shape[0]
    return pl.pallas_call(
        _gmlp_kernel,
        grid=(n // tt,),
        in_specs=[
            pl.BlockSpec((tt, GM_W), lambda i: (i, COL_GM_U // GM_W)),
            pl.BlockSpec((tt, GM_W), lambda i: (i, COL_GM_V // GM_W)),
            pl.BlockSpec((1, GM_W), lambda i: (0, 0)),
            pl.BlockSpec((GM_GROUPS, CHUNK, CHUNK), lambda i: (0, 0, 0)),
            pl.BlockSpec((CHUNK, GM_GROUPS), lambda i: (0, 0)),
        ],
        out_specs=pl.BlockSpec((tt, GM_W), lambda i: (i, 0)),
        out_shape=jax.ShapeDtypeStruct((n, GM_W), BF16),
        compiler_params=_cparams(("parallel",)),
        name="gmlp",
    )(proj, proj, vnorm_g, ws, b_t)


def _mlstm_kernel(qk_ref, v_ref, o_ref, gt_ref, cw_ref, cb_ref, gb_ref, ng_ref, out_ref,
                  xx_ref, c_ref, n_ref, m_ref):
    @pl.when(pl.program_id(1) == 0)
    def _():
        xx_ref[0:8, :] = jnp.zeros((8, ML_W), F32)
        c_ref[...] = jnp.zeros_like(c_ref)
        n_ref[...] = jnp.zeros_like(n_ref)
        m_ref[...] = jnp.zeros_like(m_ref)

    x = qk_ref[...].astype(F32)
    xx_ref[8:8 + CHUNK, :] = x
    conv = cb_ref[...]
    for j in range(ML_CONV):
        conv = conv + cw_ref[j:j + 1, :] * xx_ref[5 + j:5 + j + CHUNK, :]
    xx_ref[0:8, :] = x[CHUNK - 8:CHUNK, :]
    qk = conv * jax.nn.sigmoid(conv)

    gates = gt_ref[...] + gb_ref[...]
    ig = gates[:, :LANES]
    lf = jax.nn.log_sigmoid(gates[:, LANES:])
    row = lax.broadcasted_iota(I32, (CHUNK, CHUNK), 0)
    col = lax.broadcasted_iota(I32, (CHUNK, CHUNK), 1)
    causal = col <= row
    bcum = jnp.dot(causal.astype(F32), lf, precision=HIGHEST, preferred_element_type=F32)
    b_end = bcum[CHUNK - 1:CHUNK, :]
    m_st = m_ref[...]
    inter_all = bcum + m_st
    r_all = ig - bcum
    r_t = r_all.T
    w_log = b_end + r_all
    m_new = jnp.maximum(b_end + m_st, jnp.max(w_log, axis=0, keepdims=True))
    w_upd_all = jnp.exp(w_log - m_new)
    decay_all = jnp.exp(b_end + m_st - m_new)
    m_ref[...] = m_new

    lane = lax.broadcasted_iota(I32, (1, LANES), 1)
    sub = lax.broadcasted_iota(I32, (LANES, 1), 0)
    for h in range(ML_HEADS):
        j, half = divmod(h, 2)
        lo, hi = ML_DK * half, ML_DK * (half + 1)
        lane_sel = (lane >= lo) & (lane < hi)
        row_sel = (sub >= lo) & (sub < hi)
        hs = slice(h * LANES, (h + 1) * LANES)
        q_f = jnp.where(lane_sel, qk[:, j * LANES:(j + 1) * LANES], 0.0)
        k_f = jnp.where(lane_sel, qk[:, ML_HEADS * ML_DK + j * LANES:ML_HEADS * ML_DK + (j + 1) * LANES], 0.0) * (ML_DK ** -0.5)
        q_b = q_f.astype(BF16)
        b_col = bcum[:, h:h + 1]
        inter = inter_all[:, h:h + 1]
        d = jnp.where(causal, b_col + r_t[h:h + 1, :], -jnp.inf)
        m_t = jnp.maximum(inter, jnp.max(d, axis=-1, keepdims=True))
        w_intra = jnp.exp(d - m_t)
        w_inter = jnp.exp(inter - m_t)
        s = lax.dot_general(q_b, k_f.astype(BF16), (((1,), (1,)), ((), ())), preferred_element_type=F32) * w_intra
        v_h = v_ref[:, hs]
        c_pair = c_ref[j]
        n_pair = n_ref[j:j + 1, :]
        num = jnp.dot(s.astype(BF16), v_h, preferred_element_type=F32) + w_inter * jnp.dot(
            q_b, c_pair.astype(BF16), preferred_element_type=F32)
        den = jnp.sum(s, axis=-1, keepdims=True) + w_inter * jnp.sum(q_f * n_pair, axis=-1, keepdims=True)
        hh = num / jnp.maximum(jnp.abs(den), jnp.exp(-m_t))
        kw = k_f * w_upd_all[:, h:h + 1]
        decay = decay_all[:, h:h + 1]
        upd = lax.dot_general(kw.astype(BF16), v_h, (((0,), (0,)), ((), ())), preferred_element_type=F32)
        c_ref[j] = jnp.where(row_sel, decay * c_pair + upd, c_pair)
        n_ref[j:j + 1, :] = jnp.where(lane_sel, decay * n_pair + jnp.sum(kw, axis=0, keepdims=True), n_pair)
        hn = _rms(hh, ng_ref[:, hs])
        out_ref[:, hs] = (jax.nn.sigmoid(o_ref[:, hs].astype(F32)) * hn).astype(BF16)


def _mlstm(proj, gates, conv_w, conv_b, gate_b, norm_g, *, bsz, seq):
    n = proj.shape[0]
    nc = seq // CHUNK
    return pl.pallas_call(
        _mlstm_kernel,
        grid=(bsz, nc),
        in_specs=[
            pl.BlockSpec((CHUNK, ML_W), lambda b, c: (b * nc + c, COL_ML_QK // ML_W)),
            pl.BlockSpec((CHUNK, ML_W), lambda b, c: (b * nc + c, COL_ML_V // ML_W)),
            pl.BlockSpec((CHUNK, ML_W), lambda b, c: (b * nc + c, COL_ML_O // ML_W)),
            pl.BlockSpec((CHUNK, P_GATE), lambda b, c: (b * nc + c, 0)),
            pl.BlockSpec((ML_CONV, ML_W), lambda b, c: (0, 0)),
            pl.BlockSpec((1, ML_W), lambda b, c: (0, 0)),
            pl.BlockSpec((1, P_GATE), lambda b, c: (0, 0)),
            pl.BlockSpec((1, ML_W), lambda b, c: (0, 0)),
        ],
        out_specs=pl.BlockSpec((CHUNK, ML_W), lambda b, c: (b * nc + c, 0)),
        out_shape=jax.ShapeDtypeStruct((n, ML_W), BF16),
        scratch_shapes=[
            pltpu.VMEM((8 + CHUNK, ML_W), F32),
            pltpu.VMEM((ML_HEADS // 2, LANES, LANES), F32),
            pltpu.VMEM((8, LANES), F32),
            pltpu.VMEM((1, LANES), F32),
        ],
        compiler_params=_cparams(("parallel", "arbitrary")),
        name="mlstm",
    )(proj, proj, proj, gates, conv_w, conv_b, gate_b, norm_g)


def _qkprep_kernel(q_ref, k_ref, v_ref, qg_ref, kg_ref, cos_ref, sa_ref, sb_ref, q0t_ref, q1t_ref, ko_ref, vt_ref, *, tk):
    tt = q_ref.shape[0]
    lane = lax.broadcasted_iota(I32, (1, LANES), 1)
    map0 = lane < DA_DK
    r = lax.shift_right_logical(lax.broadcasted_iota(I32, (2 * LANES, LANES), 0), 6) & 1
    c = lax.shift_right_logical(lax.broadcasted_iota(I32, (2 * LANES, LANES), 1), 6)
    group_sum = (r == c).astype(BF16)
    cos, sa, sb = cos_ref[...], sa_ref[...], sb_ref[...]

    def norm_rope(x, g):
        sq = x * x
        sq_hi = sq.astype(BF16)
        sq_lo = (sq - sq_hi.astype(F32)).astype(BF16)
        ss = jnp.dot(jnp.concatenate([sq_hi, sq_lo], axis=1), group_sum, preferred_element_type=F32)
        xn = x * lax.rsqrt(ss * (1.0 / DA_DK) + EPS) * g
        return xn * cos + pltpu.roll(xn, LANES - ROPE_DIM // 2, 1) * sa + pltpu.roll(xn, ROPE_DIM // 2, 1) * sb

    for h in range(DA_HEADS):
        hs = slice(h * LANES, (h + 1) * LANES)
        q = norm_rope(q_ref[:, hs].astype(F32), qg_ref[...]) * (DA_DK ** -0.5 * LOG2_E)
        q0t_ref[hs, :] = jnp.where(map0, q, 0.0).T.astype(BF16)
        q1t_ref[hs, :] = jnp.where(map0, 0.0, q).T.astype(BF16)
        ko_ref[:, hs] = norm_rope(k_ref[:, hs].astype(F32), kg_ref[...]).astype(BF16)
        vt = v_ref[:, hs].astype(F32).T.astype(BF16)
        for j in range(tt // tk):
            vt_ref[h, j] = vt[:, j * tk:(j + 1) * tk]


def _qkprep(proj, qg, kg, cos_t, sa_t, sb_t, *, bsz, seq, tt, tk):
    n = proj.shape[0]
    ns = seq // tt
    tok_in = lambda col: pl.BlockSpec((tt, DA_W), lambda b, s: (b * ns + s, col // DA_W))
    qt = pl.BlockSpec((DA_W, tt), lambda b, s: (b, s))
    tab = pl.BlockSpec((tt, LANES), lambda b, s: (s, 0))
    vec = pl.BlockSpec((1, LANES), lambda b, s: (0, 0))
    return pl.pallas_call(
        functools.partial(_qkprep_kernel, tk=tk),
        grid=(bsz, ns),
        in_specs=[tok_in(COL_DA_Q), tok_in(COL_DA_K), tok_in(COL_DA_V), vec, vec, tab, tab, tab],
        out_specs=[
            qt, qt,
            pl.BlockSpec((tt, DA_W), lambda b, s: (b * ns + s, 0)),
            pl.BlockSpec((None, DA_HEADS, tt // tk, LANES, tk), lambda b, s: (b, 0, s, 0, 0)),
        ],
        out_shape=[
            jax.ShapeDtypeStruct((bsz * DA_W, seq), BF16),
            jax.ShapeDtypeStruct((bsz * DA_W, seq), BF16),
            jax.ShapeDtypeStruct((n, DA_W), BF16),
            jax.ShapeDtypeStruct((bsz, DA_HEADS, seq // tk, LANES, tk), BF16),
        ],
        compiler_params=_cparams(("parallel", "parallel")),
        name="qkprep",
    )(proj, proj, proj, qg, kg, cos_t, sa_t, sb_t)


def _flash_kernel(q0t_ref, q1t_ref, k_ref, vt_ref, lam_ref, sg_ref, o_ref, m_sc, l_sc, a_sc, acc_sc, p_sc,
                  *, tq, lambda_init):
    tk = tq
    nq = k_ref.shape[0] // tq
    lp = lam_ref[...]
    lam = (jnp.exp(jnp.sum(lp[0:1, :] * lp[1:2, :], axis=-1, keepdims=True))
           - jnp.exp(jnp.sum(lp[2:3, :] * lp[3:4, :], axis=-1, keepdims=True)) + lambda_init)
    kpos = lax.broadcasted_iota(I32, (tk, 1), 0)
    q1 = lax.broadcasted_iota(I32, (1, tq), 1)
    causal = kpos <= jnp.concatenate([q1, q1], axis=1)

    def softmax(s, m_prev, l_prev):
        m_new = jnp.maximum(m_prev, jnp.max(s, axis=0, keepdims=True))
        alpha = jnp.exp2(m_prev - m_new)
        p = jnp.exp2(s - m_new)
        return p.astype(BF16), alpha, m_new, alpha * l_prev + jnp.sum(p, axis=0, keepdims=True)

    for qi in range(nq):
        qs = slice(qi * tq, (qi + 1) * tq)
        qt = jnp.concatenate([q0t_ref[:, qs], q1t_ref[:, qs]], axis=1)

        def scores(j, qt=qt):
            start = pl.multiple_of(j * tk, tk)
            return jnp.dot(k_ref[pl.ds(start, tk), :], qt, preferred_element_type=F32)

        if qi == 0:
            s_last = jnp.where(causal, scores(0), -jnp.inf)
            m_fin = jnp.max(s_last, axis=0, keepdims=True)
            p = jnp.exp2(s_last - m_fin)
            l_fin = jnp.sum(p, axis=0, keepdims=True)
            acc = jnp.dot(vt_ref[0], p.astype(BF16), preferred_element_type=F32)
        else:
            s0 = scores(0)
            m0 = jnp.max(s0, axis=0, keepdims=True)
            p0 = jnp.exp2(s0 - m0)
            m_sc[...] = m0
            l_sc[...] = jnp.sum(p0, axis=0, keepdims=True)
            p_sc[...] = p0.astype(BF16)
            a_sc[...] = jnp.ones_like(a_sc)
            acc_sc[...] = jnp.zeros_like(acc_sc)

            def body(j, s_cur, scores=scores):
                s_next = scores(j + 1)
                acc_sc[...] = a_sc[...] * acc_sc[...] + jnp.dot(vt_ref[j - 1], p_sc[...], preferred_element_type=F32)
                p, alpha, m_new, l_new = softmax(s_cur, m_sc[...], l_sc[...])
                p_sc[...] = p
                a_sc[...] = alpha
                m_sc[...] = m_new
                l_sc[...] = l_new
                return s_next

            s_last = lax.fori_loop(1, qi, body, scores(1))
            acc_prev = a_sc[...] * acc_sc[...] + jnp.dot(vt_ref[qi - 1], p_sc[...], preferred_element_type=F32)
            p, alpha, _, l_fin = softmax(jnp.where(causal, s_last, -jnp.inf), m_sc[...], l_sc[...])
            acc = alpha * acc_prev + jnp.dot(vt_ref[qi], p, preferred_element_type=F32)

        o = acc / l_fin
        diff = o[:, :tq] - lam * o[:, tq:]
        y = diff * lax.rsqrt(jnp.mean(diff * diff, axis=0, keepdims=True) + EPS) * sg_ref[...] * (1.0 - lambda_init)
        o_ref[qs, :] = y.T.astype(BF16)


def _flash(q0t, q1t, kf, vt, lam_p, subln_col, *, bsz, seq, lambda_init, tq):
    n = bsz * seq
    qspec = pl.BlockSpec((LANES, seq), lambda b, h: (b * DA_HEADS + h, 0))
    return pl.pallas_call(
        functools.partial(_flash_kernel, tq=tq, lambda_init=lambda_init),
        grid=(bsz, DA_HEADS),
        in_specs=[
            qspec, qspec,
            pl.BlockSpec((seq, LANES), lambda b, h: (b, h)),
            pl.BlockSpec((None, None, seq // tq, LANES, tq), lambda b, h: (b, h, 0, 0, 0)),
            pl.BlockSpec((4, DA_DK), lambda b, h: (0, 0)),
            pl.BlockSpec((LANES, 1), lambda b, h: (0, 0)),
        ],
        out_specs=pl.BlockSpec((seq, LANES), lambda b, h: (b, h)),
        out_shape=jax.ShapeDtypeStruct((n, DA_W), BF16),
        scratch_shapes=[
            pltpu.VMEM((1, 2 * tq), F32),
            pltpu.VMEM((1, 2 * tq), F32),
            pltpu.VMEM((1, 2 * tq), F32),
            pltpu.VMEM((LANES, 2 * tq), F32),
            pltpu.VMEM((tq, 2 * tq), BF16),
        ],
        compiler_params=_cparams(("parallel", "parallel")),
        name="flash",
    )(q0t, q1t, kf, vt, lam_p, subln_col)


def _outproj_router_kernel(ygm_ref, yml_ref, yda_ref, x_ref, wo_gm_ref, wo_ml_ref, wo_da_ref, g2_ref, wr_ref, br_ref,
                           x1_ref, hp_ref, info_ref, cnt_ref, carry_ref):
    tm = x_ref.shape[0]

    @pl.when(pl.program_id(0) == 0)
    def _():
        carry_ref[...] = jnp.zeros_like(carry_ref)

    x1 = (x_ref[...]
          + jnp.dot(ygm_ref[...], wo_gm_ref[...], preferred_element_type=F32)
          + jnp.dot(yml_ref[...], wo_ml_ref[...], preferred_element_type=F32)
          + jnp.dot(yda_ref[...], wo_da_ref[...], preferred_element_type=F32))
    x1_ref[...] = x1
    h2 = _rms(x1, g2_ref[...])
    h2_hi = h2.astype(BF16)
    half = D_MODEL // 2
    hi = pltpu.bitcast(h2_hi[:, :half].astype(F32), U32)
    lo = pltpu.bitcast(h2_hi[:, half:].astype(F32), U32)
    hp_ref[...] = hi | (lo >> 16)

    h2_lo = (h2 - h2_hi.astype(F32)).astype(BF16)
    hw = jnp.dot(h2_hi, wr_ref[...], preferred_element_type=F32)
    logits = (hw[:, :LANES] + hw[:, LANES:]
              + jnp.dot(h2_lo, wr_ref[:, :LANES], preferred_element_type=F32) + br_ref[...])
    lane = lax.broadcasted_iota(I32, (tm, LANES), 1)
    lane_f = lane.astype(F32)
    is_g = (lane >= N_EXPERTS) & (lane < N_EXPERTS + N_GROUPS)
    lg = jnp.where(is_g, logits, -jnp.inf)
    mg = jnp.max(lg, axis=-1, keepdims=True)
    g_lane = jnp.min(jnp.where(lg == mg, lane_f, float(LANES)), axis=-1, keepdims=True)
    pg_top = 1.0 / jnp.sum(jnp.exp(lg - mg), axis=-1, keepdims=True)
    g_idx = g_lane.astype(I32) - N_EXPERTS
    in_group = (lane < N_EXPERTS) & (lax.shift_right_logical(lane, 3) == g_idx)
    le = jnp.where(in_group, logits, -jnp.inf)
    l1 = jnp.max(le, axis=-1, keepdims=True)
    e1 = jnp.min(jnp.where(le == l1, lane_f, float(LANES)), axis=-1, keepdims=True)
    hot1 = lane_f == e1
    le2 = jnp.where(hot1, -jnp.inf, le)
    l2 = jnp.max(le2, axis=-1, keepdims=True)
    e2 = jnp.min(jnp.where(le2 == l2, lane_f, float(LANES)), axis=-1, keepdims=True)
    hot2 = lane_f == e2
    e21 = jnp.exp(l2 - l1)
    w1 = pg_top / (1.0 + e21)
    w2 = pg_top * e21 / (1.0 + e21)

    both = (hot1 | hot2).astype(BF16)
    r = lax.broadcasted_iota(I32, (tm, tm), 0)
    c = lax.broadcasted_iota(I32, (tm, tm), 1)
    before = (c < r).astype(BF16)
    prefix = jnp.dot(before, both, preferred_element_type=F32) + carry_ref[0:1, :]
    rank1 = jnp.sum(jnp.where(hot1, prefix, 0.0), axis=-1, keepdims=True)
    rank2 = jnp.sum(jnp.where(hot2, prefix, 0.0), axis=-1, keepdims=True)
    total = carry_ref[0:1, :] + jnp.sum(both.astype(F32), axis=0, keepdims=True)
    carry_ref[0:1, :] = total
    cnt_ref[...] = jnp.broadcast_to(total, cnt_ref.shape)

    info = jnp.where(lane == 0, e1, 0.0)
    for idx, val in enumerate((e2, w1, w2, rank1, rank2), start=1):
        info = jnp.where(lane == idx, val, info)
    info_ref[...] = info


def _outproj_router(ygm, yml, yda, x, wo, g2, wr, br, *, tm=512):
    n = x.shape[0]
    row = lambda w: pl.BlockSpec((tm, w), lambda i: (i, 0))
    full = lambda a, b: pl.BlockSpec((a, b), lambda i: (0, 0))
    return pl.pallas_call(
        _outproj_router_kernel,
        grid=(n // tm,),
        in_specs=[
            row(GM_W), row(ML_W), row(DA_W), row(D_MODEL),
            pl.BlockSpec((GM_W, D_MODEL), lambda i: (0, 0)),
            pl.BlockSpec((ML_W, D_MODEL), lambda i: (0, 0)),
            pl.BlockSpec((DA_W, D_MODEL), lambda i: (0, 0)),
            full(1, D_MODEL), full(D_MODEL, 2 * LANES), full(1, LANES),
        ],
        out_specs=[row(D_MODEL), row(D_MODEL // 2), row(LANES), full(8, LANES)],
        out_shape=[
            jax.ShapeDtypeStruct((n, D_MODEL), F32),
            jax.ShapeDtypeStruct((n, D_MODEL // 2), U32),
            jax.ShapeDtypeStruct((n, LANES), F32),
            jax.ShapeDtypeStruct((8, LANES), F32),
        ],
        scratch_shapes=[pltpu.VMEM((8, LANES), F32)],
        compiler_params=_cparams(("arbitrary",)),
        name="outproj_router",
    )(ygm, yml, yda, x, wo[0], wo[1], wo[2], g2, wr, br)


ROW_DMA_UNROLL = 8


def _row_dmas(tt, row_copy):
    def issue(r, carry):
        for k in range(2):
            row_copy(r, k).start(priority=k)
        return carry

    def drain(r, carry):
        for k in range(2):
            row_copy(r, k).wait()
        return carry

    lax.fori_loop(0, tt, issue, 0, unroll=ROW_DMA_UNROLL)
    lax.fori_loop(0, tt, drain, 0, unroll=ROW_DMA_UNROLL)


def _dispatch_kernel(pos0_ref, pos1_ref, h_ref, xs_ref, sem):
    pos = (pos0_ref, pos1_ref)
    _row_dmas(h_ref.shape[0],
              lambda r, k: pltpu.make_async_copy(h_ref.at[pl.ds(r, 1)], xs_ref.at[pl.ds(pos[k][r], 1)], sem))


def _dispatch(pos0, pos1, hp, *, tt=256):
    n, w = hp.shape
    idx = pl.BlockSpec((tt,), lambda i: (i,), memory_space=pltpu.SMEM)
    return pl.pallas_call(
        _dispatch_kernel,
        grid=(n // tt,),
        in_specs=[idx, idx, pl.BlockSpec((tt, w), lambda i: (i, 0))],
        out_specs=pl.BlockSpec(memory_space=pl.ANY),
        out_shape=jax.ShapeDtypeStruct((2 * n, w), U32),
        scratch_shapes=[pltpu.SemaphoreType.DMA(())],
        compiler_params=_cparams(("arbitrary",)),
        name="dispatch",
    )(pos0, pos1, hp)


def _gmm_kernel(tile_ref, exp_ref, lo_ref, hi_ref, first_ref, newexp_ref,
                xs_ref, wg_ref, wu_ref, wd_ref, o_ref, wg_sc, wu_sc, wd_sc):
    i = pl.program_id(0)
    tm = xs_ref.shape[0]

    @pl.when(newexp_ref[i] == 1)
    def _():
        wg_sc[...] = wg_ref[...].astype(BF16)
        wu_sc[...] = wu_ref[...].astype(BF16)
        wd_sc[...] = wd_ref[...].astype(BF16)

    @pl.when(first_ref[i] == 1)
    def _():
        o_ref[...] = jnp.zeros_like(o_ref)

    lo, hi = lo_ref[i], hi_ref[i]

    @pl.when(hi > lo)
    def _():
        half = D_MODEL // 2
        packed = xs_ref[...]
        xa = pltpu.bitcast(packed & jnp.uint32(0xFFFF0000), F32).astype(BF16)
        xb = pltpu.bitcast(packed << 16, F32).astype(BF16)
        g = (jnp.dot(xa, wg_sc[0:half, :], preferred_element_type=F32)
             + jnp.dot(xb, wg_sc[half:, :], preferred_element_type=F32))
        u = (jnp.dot(xa, wu_sc[0:half, :], preferred_element_type=F32)
             + jnp.dot(xb, wu_sc[half:, :], preferred_element_type=F32))
        hdn = (g * jax.nn.sigmoid(g) * u).astype(BF16)
        y = jnp.dot(hdn, wd_sc[...], preferred_element_type=F32)
        y_hi = pltpu.bitcast(y[:, :half].astype(BF16).astype(F32), U32)
        y_lo = pltpu.bitcast(y[:, half:].astype(BF16).astype(F32), U32)
        rows = lax.broadcasted_iota(I32, (tm, 1), 0) + tile_ref[i] * tm
        o_ref[...] = jnp.where((rows >= lo) & (rows < hi), y_hi | (y_lo >> 16), o_ref[...])


def _gmm(meta, xs, w_gate, w_up, w_down, layer, *, tm):
    rows = xs.shape[0]
    n_items = meta[0].shape[0]
    wspec = lambda a, b: pl.BlockSpec((None, None, a, b), lambda i, tile, exp, *_: (layer, exp[i], 0, 0))
    grid_spec = pltpu.PrefetchScalarGridSpec(
        num_scalar_prefetch=6,
        grid=(n_items,),
        in_specs=[
            pl.BlockSpec((tm, D_MODEL // 2), lambda i, tile, *_: (tile[i], 0)),
            wspec(D_MODEL, D_EXPERT), wspec(D_MODEL, D_EXPERT), wspec(D_EXPERT, D_MODEL),
        ],
        out_specs=pl.BlockSpec((tm, D_MODEL // 2), lambda i, tile, *_: (tile[i], 0)),
        scratch_shapes=[
            pltpu.VMEM((D_MODEL, D_EXPERT), BF16),
            pltpu.VMEM((D_MODEL, D_EXPERT), BF16),
            pltpu.VMEM((D_EXPERT, D_MODEL), BF16),
        ],
    )
    return pl.pallas_call(
        _gmm_kernel,
        grid_spec=grid_spec,
        out_shape=jax.ShapeDtypeStruct((rows, D_MODEL // 2), U32),
        compiler_params=_cparams(("arbitrary",)),
        name="gmm",
    )(*meta, xs, w_gate, w_up, w_down)


def _gmm_items(counts, *, rows, tm):
    n_items = rows // tm + N_EXPERTS - 1
    ends = jnp.cumsum(counts)
    starts = ends - counts
    first_tile = starts // tm
    n_tiles = jnp.where(counts > 0, (ends - 1) // tm - first_tile + 1, 0)
    item_end = jnp.cumsum(n_tiles)
    item_start = item_end - n_tiles
    total = item_end[-1]
    idx = jnp.arange(n_items, dtype=I32)
    valid = idx < total
    last = jnp.maximum(total - 1, 0)
    src = jnp.where(valid, idx, last)
    exp = jnp.minimum(jnp.sum((item_end[None, :] <= src[:, None]).astype(I32), axis=1), N_EXPERTS - 1)
    tile = (first_tile[exp] + src - item_start[exp]).astype(I32)
    lo = jnp.where(valid, starts[exp], 0).astype(I32)
    hi = jnp.where(valid, ends[exp], 0).astype(I32)
    prev_tile = jnp.concatenate([jnp.full((1,), -1, I32), tile[:-1]])
    prev_exp = jnp.concatenate([jnp.full((1,), -1, I32), exp[:-1]])
    first = (valid & (tile != prev_tile)).astype(I32)
    newexp = (valid & (exp != prev_exp)).astype(I32)
    return tile, exp, lo, hi, first, newexp


def _combine_kernel(pos0_ref, pos1_ref, x_ref, info_ref, ys_ref, o_ref, gbuf, sem):
    pos = (pos0_ref, pos1_ref)
    _row_dmas(x_ref.shape[0],
              lambda r, k: pltpu.make_async_copy(ys_ref.at[pl.ds(pos[k][r], 1)], gbuf.at[k, pl.ds(r, 1)], sem))
    half = D_MODEL // 2
    w = (info_ref[:, 2:3], info_ref[:, 3:4])
    upper = lambda k: pltpu.bitcast(gbuf[k] & jnp.uint32(0xFFFF0000), F32)
    lower = lambda k: pltpu.bitcast(gbuf[k] << 16, F32)
    o_ref[:, :half] = x_ref[:, :half] + w[0] * upper(0) + w[1] * upper(1)
    o_ref[:, half:] = x_ref[:, half:] + w[0] * lower(0) + w[1] * lower(1)


def _combine(pos0, pos1, x1, info, ys, *, tt=256):
    n = x1.shape[0]
    idx = pl.BlockSpec((tt,), lambda i: (i,), memory_space=pltpu.SMEM)
    return pl.pallas_call(
        _combine_kernel,
        grid=(n // tt,),
        in_specs=[
            idx, idx,
            pl.BlockSpec((tt, D_MODEL), lambda i: (i, 0)),
            pl.BlockSpec((tt, LANES), lambda i: (i, 0)),
            pl.BlockSpec(memory_space=pl.ANY),
        ],
        out_specs=pl.BlockSpec((tt, D_MODEL), lambda i: (i, 0)),
        out_shape=jax.ShapeDtypeStruct((n, D_MODEL), F32),
        scratch_shapes=[pltpu.VMEM((2, tt, D_MODEL // 2), U32), pltpu.SemaphoreType.DMA(())],
        compiler_params=_cparams(("arbitrary",)),
        name="combine",
    )(pos0, pos1, x1, info, ys)


def _rope_tables(seq):
    pos = jnp.arange(seq, dtype=F32)
    inv_freq = ROPE_THETA ** (-jnp.arange(0, ROPE_DIM, 2, dtype=F32) / ROPE_DIM)
    ang = pos[:, None] * inv_freq[None, :]
    cos, sin = jnp.cos(ang), jnp.sin(ang)
    half = ROPE_DIM // 2
    z = lambda w: jnp.zeros((seq, w), F32)
    rest = DA_DK - ROPE_DIM
    cos_t = jnp.concatenate([cos, cos, jnp.ones((seq, rest), F32)], axis=1)
    sa_t = jnp.concatenate([-sin, z(half), z(rest)], axis=1)
    sb_t = jnp.concatenate([z(half), sin, z(rest)], axis=1)
    tile2 = lambda t: jnp.concatenate([t, t], axis=1)
    return tile2(cos_t), tile2(sa_t), tile2(sb_t)


def kernel(x, norm1_g, w_in, gm_vnorm_g, gm_ws, gm_b, ml_conv_w, ml_conv_b, ml_ig_b, ml_fg_b, ml_norm_g, da_qnorm_g, da_knorm_g, da_lambda, da_subln_g, w_out, norm2_g, moe_w_rg, moe_b_rg, moe_w_re, moe_b_re, moe_w_gate, moe_w_up, moe_w_down):
    bsz, seq, d = x.shape
    n = bsz * seq
    depth = w_in.shape[0]
    gmm_tm = 512
    tt_rows = 256
    flash_tq = min(1024, seq)
    cos_t, sa_t, sb_t = _rope_tables(seq)
    xf = x.reshape(n, d)
    row = lambda v: v.reshape(1, -1).astype(F32)
    lane_pad = lambda v, w: jnp.concatenate([v, jnp.zeros((w - v.shape[0],), v.dtype)])
    w_main, w_gate = _wprep(w_in)
    for l in range(depth):
        lambda_init = 0.8 - 0.6 * math.exp(-0.3 * l)
        proj, gates = _inproj(xf, row(norm1_g[l]), w_main, w_gate, l)

        y_gm = _gmlp(proj, row(gm_vnorm_g[l]), gm_ws[l], gm_b[l].T)

        gate_b = jnp.concatenate([lane_pad(ml_ig_b[l], LANES), lane_pad(ml_fg_b[l], LANES)]).reshape(1, P_GATE)
        y_ml = _mlstm(proj, gates, ml_conv_w[l], row(ml_conv_b[l]), gate_b, row(ml_norm_g[l]), bsz=bsz, seq=seq)

        qg = jnp.tile(da_qnorm_g[l], 2).reshape(1, LANES)
        kg = jnp.tile(da_knorm_g[l], 2).reshape(1, LANES)
        q0t, q1t, kf, vt = _qkprep(proj, qg, kg, cos_t, sa_t, sb_t, bsz=bsz, seq=seq, tt=flash_tq, tk=flash_tq)
        y_da = _flash(q0t, q1t, kf, vt, da_lambda[l], da_subln_g[l].reshape(LANES, 1), bsz=bsz, seq=seq,
                      lambda_init=lambda_init, tq=flash_tq)

        wo = w_out[l].astype(BF16)
        wo_parts = (wo[:GM_W], wo[GM_W:GM_W + ML_W], wo[GM_W + ML_W:])
        wr = jnp.concatenate([moe_w_re[l], moe_w_rg[l], jnp.zeros((d, LANES - N_EXPERTS - N_GROUPS), F32)], axis=1)
        wr_hi = wr.astype(BF16)
        wr_lo = (wr - wr_hi.astype(F32)).astype(BF16)
        br = lane_pad(jnp.concatenate([moe_b_re[l], moe_b_rg[l]]), LANES).reshape(1, LANES)
        x1, hp, info, cnt = _outproj_router(y_gm, y_ml, y_da, xf, wo_parts, row(norm2_g[l]),
                                            jnp.concatenate([wr_hi, wr_lo], axis=1), br)

        counts = cnt[0, :N_EXPERTS].astype(I32)
        offsets = jnp.cumsum(counts) - counts
        e12 = info[:, 0:2].astype(I32)
        pos = offsets[e12] + info[:, 4:6].astype(I32)
        pos0, pos1 = pos[:, 0], pos[:, 1]
        meta = _gmm_items(counts, rows=2 * n, tm=gmm_tm)

        xs = _dispatch(pos0, pos1, hp, tt=tt_rows)
        ys = _gmm(meta, xs, moe_w_gate, moe_w_up, moe_w_down, l, tm=gmm_tm)
        xf = _combine(pos0, pos1, x1, info, ys, tt=tt_rows)
    return xf.reshape(bsz, seq, d)
```

```python
import functools
import math

import jax
import jax.numpy as jnp
from jax import lax
from jax.experimental import pallas as pl
from jax.experimental.pallas import tpu as pltpu

F32 = jnp.float32
BF16 = jnp.bfloat16
I32 = jnp.int32
U32 = jnp.uint32
HIGHEST = lax.Precision.HIGHEST
LOG2_E = math.log2(math.e)

LANES = 128
D_MODEL = 2048
CHUNK = 128
GM_GROUPS = 4
GM_W = 512
ML_HEADS = 6
ML_DK = 64
ML_W = 768
ML_CONV = 4
DA_HEADS = 6
DA_DK = 64
DA_W = 768
ROPE_THETA = 500000.0
ROPE_DIM = 16
N_GROUPS = 4
EXP_PER_GROUP = 8
N_EXPERTS = 32
D_EXPERT = 512
EPS = 1e-6

COL_DA_Q, COL_DA_K, COL_DA_V = 0, 768, 1536
COL_ML_V, COL_ML_O, COL_ML_QK = 2304, 3072, 3840
COL_GM_U, COL_GM_V = 4608, 5120
P_MAIN = 5632
P_GATE = 256

VMEM_LIMIT = 56 * 1024 * 1024


def _cparams(sem):
    return pltpu.CompilerParams(dimension_semantics=sem, vmem_limit_bytes=VMEM_LIMIT)


def _gelu(x):
    return 0.5 * x * (1.0 + lax.erf(x * (1.0 / math.sqrt(2.0))))


def _rms(x, g):
    return x * lax.rsqrt(jnp.mean(x * x, axis=-1, keepdims=True) + EPS) * g


_W_IN_MOVES = (
    (3340, 2304, COL_DA_Q),
    (1792, 1536, COL_ML_V),
    (1024, 768, COL_ML_QK),
    (0, 1024, COL_GM_U),
)
_W_IN_IG, _W_IN_FG = 3328, 3334


def _wprep_kernel(w_ref, main_ref, gate_ref):
    for src, width, dst in _W_IN_MOVES:
        main_ref[:, dst:dst + width] = w_ref[:, src:src + width].astype(BF16)
    gate_ref[...] = jnp.zeros_like(gate_ref)
    gate_ref[:, 0:ML_HEADS] = w_ref[:, _W_IN_IG:_W_IN_IG + ML_HEADS].astype(BF16)
    gate_ref[:, LANES:LANES + ML_HEADS] = w_ref[:, _W_IN_FG:_W_IN_FG + ML_HEADS].astype(BF16)


def _wprep(w_in, *, tr=256):
    depth, d, d_in = w_in.shape
    return pl.pallas_call(
        _wprep_kernel,
        grid=(depth, d // tr),
        in_specs=[pl.BlockSpec((None, tr, d_in), lambda l, i: (l, i, 0))],
        out_specs=[
            pl.BlockSpec((None, tr, P_MAIN), lambda l, i: (l, i, 0)),
            pl.BlockSpec((None, tr, P_GATE), lambda l, i: (l, i, 0)),
        ],
        out_shape=[
            jax.ShapeDtypeStruct((depth, d, P_MAIN), BF16),
            jax.ShapeDtypeStruct((depth, d, P_GATE), BF16),
        ],
        compiler_params=_cparams(("parallel", "parallel")),
        name="wprep",
    )(w_in)


def _inproj_kernel(x_ref, g_ref, w_ref, wg_ref, o_ref, og_ref):
    h = _rms(x_ref[...], g_ref[...]).astype(BF16)
    og_ref[...] = jnp.dot(h, wg_ref[...], preferred_element_type=F32)
    o_ref[...] = jnp.dot(h, w_ref[...], preferred_element_type=F32).astype(BF16)


def _inproj(x, g, w_main, w_gate, layer, *, tm=512):
    n = x.shape[0]
    resident = dict(pipeline_mode=pl.Buffered(1))
    return pl.pallas_call(
        _inproj_kernel,
        grid=(n // tm,),
        in_specs=[
            pl.BlockSpec((tm, D_MODEL), lambda i: (i, 0)),
            pl.BlockSpec((1, D_MODEL), lambda i: (0, 0)),
            pl.BlockSpec((None, D_MODEL, P_MAIN), lambda i: (layer, 0, 0), **resident),
            pl.BlockSpec((None, D_MODEL, P_GATE), lambda i: (layer, 0, 0), **resident),
        ],
        out_specs=[
            pl.BlockSpec((tm, P_MAIN), lambda i: (i, 0)),
            pl.BlockSpec((tm, P_GATE), lambda i: (i, 0)),
        ],
        out_shape=[
            jax.ShapeDtypeStruct((n, P_MAIN), BF16),
            jax.ShapeDtypeStruct((n, P_GATE), F32),
        ],
        compiler_params=_cparams(("parallel",)),
        name="inproj",
    )(x, g, w_main, w_gate)


def _gmlp_kernel(u_ref, v_ref, vg_ref, ws_ref, bt_ref, o_ref):
    tt = u_ref.shape[0]
    row = lax.broadcasted_iota(I32, (CHUNK, CHUNK), 0)
    col = lax.broadcasted_iota(I32, (CHUNK, CHUNK), 1)
    causal = col <= row
    for g in range(GM_GROUPS):
        cs = slice(g * LANES, (g + 1) * LANES)
        wsg = jnp.where(causal, ws_ref[g], 0.0).astype(BF16)
        bcol = bt_ref[:, g:g + 1]
        for c in range(tt // CHUNK):
            rs = slice(c * CHUNK, (c + 1) * CHUNK)
            v = _gelu(v_ref[rs, cs].astype(F32))
            vn = _rms(v, vg_ref[:, cs]).astype(BF16)
            z = jnp.dot(wsg, vn, preferred_element_type=F32) + bcol
            u = _gelu(u_ref[rs, cs].astype(F32))
            o_ref[rs, cs] = (u * z).astype(BF16)


def _gmlp(proj, vnorm_g, ws, b_t, *, tt=512):
    n = proj.shape[0]
    return pl.pallas_call(
        _gmlp_kernel,
        grid=(n // tt,),
        in_specs=[
            pl.BlockSpec((tt, GM_W), lambda i: (i, COL_GM_U // GM_W)),
            pl.BlockSpec((tt, GM_W), lambda i: (i, COL_GM_V // GM_W)),
            pl.BlockSpec((1, GM_W), lambda i: (0, 0)),
            pl.BlockSpec((GM_GROUPS, CHUNK, CHUNK), lambda i: (0, 0, 0)),
            pl.BlockSpec((CHUNK, GM_GROUPS), lambda i: (0, 0)),
        ],
        out_specs=pl.BlockSpec((tt, GM_W), lambda i: (i, 0)),
        out_shape=jax.ShapeDtypeStruct((n, GM_W), BF16),
        compiler_params=_cparams(("parallel",)),
        name="gmlp",
    )(proj, proj, vnorm_g, ws, b_t)


def _mlstm_kernel(qk_ref, v_ref, o_ref, gt_ref, cw_ref, cb_ref, gb_ref, ng_ref, out_ref,
                  xx_ref, c_ref, n_ref, m_ref):
    @pl.when(pl.program_id(1) == 0)
    def _():
        xx_ref[0:8, :] = jnp.zeros((8, ML_W), F32)
        c_ref[...] = jnp.zeros_like(c_ref)
        n_ref[...] = jnp.zeros_like(n_ref)
        m_ref[...] = jnp.zeros_like(m_ref)

    x = qk_ref[...].astype(F32)
    xx_ref[8:8 + CHUNK, :] = x
    conv = cb_ref[...]
    for j in range(ML_CONV):
        conv = conv + cw_ref[j:j + 1, :] * xx_ref[5 + j:5 + j + CHUNK, :]
    xx_ref[0:8, :] = x[CHUNK - 8:CHUNK, :]
    qk = conv * jax.nn.sigmoid(conv)

    gates = gt_ref[...] + gb_ref[...]
    ig = gates[:, :LANES]
    lf = jax.nn.log_sigmoid(gates[:, LANES:])
    row = lax.broadcasted_iota(I32, (CHUNK, CHUNK), 0)
    col = lax.broadcasted_iota(I32, (CHUNK, CHUNK), 1)
    causal = col <= row
    bcum = jnp.dot(causal.astype(F32), lf, precision=HIGHEST, preferred_element_type=F32)
    b_end = bcum[CHUNK - 1:CHUNK, :]
    m_st = m_ref[...]
    inter_all = bcum + m_st
    r_all = ig - bcum
    r_t = r_all.T
    w_log = b_end + r_all
    m_new = jnp.maximum(b_end + m_st, jnp.max(w_log, axis=0, keepdims=True))
    w_upd_all = jnp.exp(w_log - m_new)
    decay_all = jnp.exp(b_end + m_st - m_new)
    m_ref[...] = m_new

    lane = lax.broadcasted_iota(I32, (1, LANES), 1)
    sub = lax.broadcasted_iota(I32, (LANES, 1), 0)
    for h in range(ML_HEADS):
        j, half = divmod(h, 2)
        lo, hi = ML_DK * half, ML_DK * (half + 1)
        lane_sel = (lane >= lo) & (lane < hi)
        row_sel = (sub >= lo) & (sub < hi)
        hs = slice(h * LANES, (h + 1) * LANES)
        q_f = jnp.where(lane_sel, qk[:, j * LANES:(j + 1) * LANES], 0.0)
        k_f = jnp.where(lane_sel, qk[:, ML_HEADS * ML_DK + j * LANES:ML_HEADS * ML_DK + (j + 1) * LANES], 0.0) * (ML_DK ** -0.5)
        q_b = q_f.astype(BF16)
        b_col = bcum[:, h:h + 1]
        inter = inter_all[:, h:h + 1]
        d = jnp.where(causal, b_col + r_t[h:h + 1, :], -jnp.inf)
        m_t = jnp.maximum(inter, jnp.max(d, axis=-1, keepdims=True))
        w_intra = jnp.exp(d - m_t)
        w_inter = jnp.exp(inter - m_t)
        s = lax.dot_general(q_b, k_f.astype(BF16), (((1,), (1,)), ((), ())), preferred_element_type=F32) * w_intra
        v_h = v_ref[:, hs]
        c_pair = c_ref[j]
        n_pair = n_ref[j:j + 1, :]
        num = jnp.dot(s.astype(BF16), v_h, preferred_element_type=F32) + w_inter * jnp.dot(
            q_b, c_pair.astype(BF16), preferred_element_type=F32)
        den = jnp.sum(s, axis=-1, keepdims=True) + w_inter * jnp.sum(q_f * n_pair, axis=-1, keepdims=True)
        hh = num / jnp.maximum(jnp.abs(den), jnp.exp(-m_t))
        kw = k_f * w_upd_all[:, h:h + 1]
        decay = decay_all[:, h:h + 1]
        upd = lax.dot_general(kw.astype(BF16), v_h, (((0,), (0,)), ((), ())), preferred_element_type=F32)
        c_ref[j] = jnp.where(row_sel, decay * c_pair + upd, c_pair)
        n_ref[j:j + 1, :] = jnp.where(lane_sel, decay * n_pair + jnp.sum(kw, axis=0, keepdims=True), n_pair)
        hn = _rms(hh, ng_ref[:, hs])
        out_ref[:, hs] = (jax.nn.sigmoid(o_ref[:, hs].astype(F32)) * hn).astype(BF16)


def _mlstm(proj, gates, conv_w, conv_b, gate_b, norm_g, *, bsz, seq):
    n = proj.shape[0]
    nc = seq // CHUNK
    return pl.pallas_call(
        _mlstm_kernel,
        grid=(bsz, nc),
        in_specs=[
            pl.BlockSpec((CHUNK, ML_W), lambda b, c: (b * nc + c, COL_ML_QK // ML_W)),
            pl.BlockSpec((CHUNK, ML_W), lambda b, c: (b * nc + c, COL_ML_V // ML_W)),
            pl.BlockSpec((CHUNK, ML_W), lambda b, c: (b * nc + c, COL_ML_O // ML_W)),
            pl.BlockSpec((CHUNK, P_GATE), lambda b, c: (b * nc + c, 0)),
            pl.BlockSpec((ML_CONV, ML_W), lambda b, c: (0, 0)),
            pl.BlockSpec((1, ML_W), lambda b, c: (0, 0)),
            pl.BlockSpec((1, P_GATE), lambda b, c: (0, 0)),
            pl.BlockSpec((1, ML_W), lambda b, c: (0, 0)),
        ],
        out_specs=pl.BlockSpec((CHUNK, ML_W), lambda b, c: (b * nc + c, 0)),
        out_shape=jax.ShapeDtypeStruct((n, ML_W), BF16),
        scratch_shapes=[
            pltpu.VMEM((8 + CHUNK, ML_W), F32),
            pltpu.VMEM((ML_HEADS // 2, LANES, LANES), F32),
            pltpu.VMEM((8, LANES), F32),
            pltpu.VMEM((1, LANES), F32),
        ],
        compiler_params=_cparams(("parallel", "arbitrary")),
        name="mlstm",
    )(proj, proj, proj, gates, conv_w, conv_b, gate_b, norm_g)


def _qkprep_kernel(q_ref, k_ref, v_ref, qg_ref, kg_ref, cos_ref, sa_ref, sb_ref, q0t_ref, q1t_ref, ko_ref, vt_ref, *, tk):
    tt = q_ref.shape[0]
    lane = lax.broadcasted_iota(I32, (1, LANES), 1)
    map0 = lane < DA_DK
    r = lax.shift_right_logical(lax.broadcasted_iota(I32, (2 * LANES, LANES), 0), 6) & 1
    c = lax.shift_right_logical(lax.broadcasted_iota(I32, (2 * LANES, LANES), 1), 6)
    group_sum = (r == c).astype(BF16)
    cos, sa, sb = cos_ref[...], sa_ref[...], sb_ref[...]

    def norm_rope(x, g):
        sq = x * x
        sq_hi = sq.astype(BF16)
        sq_lo = (sq - sq_hi.astype(F32)).astype(BF16)
        ss = jnp.dot(jnp.concatenate([sq_hi, sq_lo], axis=1), group_sum, preferred_element_type=F32)
        xn = x * lax.rsqrt(ss * (1.0 / DA_DK) + EPS) * g
        return xn * cos + pltpu.roll(xn, LANES - ROPE_DIM // 2, 1) * sa + pltpu.roll(xn, ROPE_DIM // 2, 1) * sb

    for h in range(DA_HEADS):
        hs = slice(h * LANES, (h + 1) * LANES)
        q = norm_rope(q_ref[:, hs].astype(F32), qg_ref[...]) * (DA_DK ** -0.5 * LOG2_E)
        q0t_ref[hs, :] = jnp.where(map0, q, 0.0).T.astype(BF16)
        q1t_ref[hs, :] = jnp.where(map0, 0.0, q).T.astype(BF16)
        ko_ref[:, hs] = norm_rope(k_ref[:, hs].astype(F32), kg_ref[...]).astype(BF16)
        vt = v_ref[:, hs].astype(F32).T.astype(BF16)
        for j in range(tt // tk):
            vt_ref[h, j] = vt[:, j * tk:(j + 1) * tk]


def _qkprep(proj, qg, kg, cos_t, sa_t, sb_t, *, bsz, seq, tt, tk):
    n = proj.shape[0]
    ns = seq // tt
    tok_in = lambda col: pl.BlockSpec((tt, DA_W), lambda b, s: (b * ns + s, col // DA_W))
    qt = pl.BlockSpec((DA_W, tt), lambda b, s: (b, s))
    tab = pl.BlockSpec((tt, LANES), lambda b, s: (s, 0))
    vec = pl.BlockSpec((1, LANES), lambda b, s: (0, 0))
    return pl.pallas_call(
        functools.partial(_qkprep_kernel, tk=tk),
        grid=(bsz, ns),
        in_specs=[tok_in(COL_DA_Q), tok_in(COL_DA_K), tok_in(COL_DA_V), vec, vec, tab, tab, tab],
        out_specs=[
            qt, qt,
            pl.BlockSpec((tt, DA_W), lambda b, s: (b * ns + s, 0)),
            pl.BlockSpec((None, DA_HEADS, tt // tk, LANES, tk), lambda b, s: (b, 0, s, 0, 0)),
        ],
        out_shape=[
            jax.ShapeDtypeStruct((bsz * DA_W, seq), BF16),
            jax.ShapeDtypeStruct((bsz * DA_W, seq), BF16),
            jax.ShapeDtypeStruct((n, DA_W), BF16),
            jax.ShapeDtypeStruct((bsz, DA_HEADS, seq // tk, LANES, tk), BF16),
        ],
        compiler_params=_cparams(("parallel", "parallel")),
        name="qkprep",
    )(proj, proj, proj, qg, kg, cos_t, sa_t, sb_t)


def _flash_kernel(q0t_ref, q1t_ref, k_ref, vt_ref, lam_ref, sg_ref, o_ref, m_sc, l_sc, a_sc, acc_sc, p_sc,
                  *, tq, lambda_init):
    tk = tq
    nq = k_ref.shape[0] // tq
    lp = lam_ref[...]
    lam = (jnp.exp(jnp.sum(lp[0:1, :] * lp[1:2, :], axis=-1, keepdims=True))
           - jnp.exp(jnp.sum(lp[2:3, :] * lp[3:4, :], axis=-1, keepdims=True)) + lambda_init)
    kpos = lax.broadcasted_iota(I32, (tk, 1), 0)
    q1 = lax.broadcasted_iota(I32, (1, tq), 1)
    causal = kpos <= jnp.concatenate([q1, q1], axis=1)

    def softmax(s, m_prev, l_prev):
        m_new = jnp.maximum(m_prev, jnp.max(s, axis=0, keepdims=True))
        alpha = jnp.exp2(m_prev - m_new)
        p = jnp.exp2(s - m_new)
        return p.astype(BF16), alpha, m_new, alpha * l_prev + jnp.sum(p, axis=0, keepdims=True)

    for qi in range(nq):
        qs = slice(qi * tq, (qi + 1) * tq)
        qt = jnp.concatenate([q0t_ref[:, qs], q1t_ref[:, qs]], axis=1)

        def scores(j, qt=qt):
            start = pl.multiple_of(j * tk, tk)
            return jnp.dot(k_ref[pl.ds(start, tk), :], qt, preferred_element_type=F32)

        if qi == 0:
            s_last = jnp.where(causal, scores(0), -jnp.inf)
            m_fin = jnp.max(s_last, axis=0, keepdims=True)
            p = jnp.exp2(s_last - m_fin)
            l_fin = jnp.sum(p, axis=0, keepdims=True)
            acc = jnp.dot(vt_ref[0], p.astype(BF16), preferred_element_type=F32)
        else:
            s0 = scores(0)
            m0 = jnp.max(s0, axis=0, keepdims=True)
            p0 = jnp.exp2(s0 - m0)
            m_sc[...] = m0
            l_sc[...] = jnp.sum(p0, axis=0, keepdims=True)
            p_sc[...] = p0.astype(BF16)
            a_sc[...] = jnp.ones_like(a_sc)
            acc_sc[...] = jnp.zeros_like(acc_sc)

            def body(j, s_cur, scores=scores):
                s_next = scores(j + 1)
                acc_sc[...] = a_sc[...] * acc_sc[...] + jnp.dot(vt_ref[j - 1], p_sc[...], preferred_element_type=F32)
                p, alpha, m_new, l_new = softmax(s_cur, m_sc[...], l_sc[...])
                p_sc[...] = p
                a_sc[...] = alpha
                m_sc[...] = m_new
                l_sc[...] = l_new
                return s_next

            s_last = lax.fori_loop(1, qi, body, scores(1))
            acc_prev = a_sc[...] * acc_sc[...] + jnp.dot(vt_ref[qi - 1], p_sc[...], preferred_element_type=F32)
            p, alpha, _, l_fin = softmax(jnp.where(causal, s_last, -jnp.inf), m_sc[...], l_sc[...])
            acc = alpha * acc_prev + jnp.dot(vt_ref[qi], p, preferred_element_type=F32)

        o = acc / l_fin
        diff = o[:, :tq] - lam * o[:, tq:]
        y = diff * lax.rsqrt(jnp.mean(diff * diff, axis=0, keepdims=True) + EPS) * sg_ref[...] * (1.0 - lambda_init)
        o_ref[qs, :] = y.T.astype(BF16)


def _flash(q0t, q1t, kf, vt, lam_p, subln_col, *, bsz, seq, lambda_init, tq):
    n = bsz * seq
    qspec = pl.BlockSpec((LANES, seq), lambda b, h: (b * DA_HEADS + h, 0))
    return pl.pallas_call(
        functools.partial(_flash_kernel, tq=tq, lambda_init=lambda_init),
        grid=(bsz, DA_HEADS),
        in_specs=[
            qspec, qspec,
            pl.BlockSpec((seq, LANES), lambda b, h: (b, h)),
            pl.BlockSpec((None, None, seq // tq, LANES, tq), lambda b, h: (b, h, 0, 0, 0)),
            pl.BlockSpec((4, DA_DK), lambda b, h: (0, 0)),
            pl.BlockSpec((LANES, 1), lambda b, h: (0, 0)),
        ],
        out_specs=pl.BlockSpec((seq, LANES), lambda b, h: (b, h)),
        out_shape=jax.ShapeDtypeStruct((n, DA_W), BF16),
        scratch_shapes=[
            pltpu.VMEM((1, 2 * tq), F32),
            pltpu.VMEM((1, 2 * tq), F32),
            pltpu.VMEM((1, 2 * tq), F32),
            pltpu.VMEM((LANES, 2 * tq), F32),
            pltpu.VMEM((tq, 2 * tq), BF16),
        ],
        compiler_params=_cparams(("parallel", "parallel")),
        name="flash",
    )(q0t, q1t, kf, vt, lam_p, subln_col)


def _outproj_router_kernel(ygm_ref, yml_ref, yda_ref, x_ref, wo_gm_ref, wo_ml_ref, wo_da_ref, g2_ref, wr_ref, br_ref,
                           x1_ref, hp_ref, info_ref, cnt_ref, carry_ref):
    tm = x_ref.shape[0]

    @pl.when(pl.program_id(0) == 0)
    def _():
        carry_ref[...] = jnp.zeros_like(carry_ref)

    x1 = (x_ref[...]
          + jnp.dot(ygm_ref[...], wo_gm_ref[...], preferred_element_type=F32)
          + jnp.dot(yml_ref[...], wo_ml_ref[...], preferred_element_type=F32)
          + jnp.dot(yda_ref[...], wo_da_ref[...], preferred_element_type=F32))
    x1_ref[...] = x1
    h2 = _rms(x1, g2_ref[...])
    h2_hi = h2.astype(BF16)
    half = D_MODEL // 2
    hi = pltpu.bitcast(h2_hi[:, :half].astype(F32), U32)
    lo = pltpu.bitcast(h2_hi[:, half:].astype(F32), U32)
    hp_ref[...] = hi | (lo >> 16)

    h2_lo = (h2 - h2_hi.astype(F32)).astype(BF16)
    hw = jnp.dot(h2_hi, wr_ref[...], preferred_element_type=F32)
    logits = (hw[:, :LANES] + hw[:, LANES:]
              + jnp.dot(h2_lo, wr_ref[:, :LANES], preferred_element_type=F32) + br_ref[...])
    lane = lax.broadcasted_iota(I32, (tm, LANES), 1)
    lane_f = lane.astype(F32)
    is_g = (lane >= N_EXPERTS) & (lane < N_EXPERTS + N_GROUPS)
    lg = jnp.where(is_g, logits, -jnp.inf)
    mg = jnp.max(lg, axis=-1, keepdims=True)
    g_lane = jnp.min(jnp.where(lg == mg, lane_f, float(LANES)), axis=-1, keepdims=True)
    pg_top = 1.0 / jnp.sum(jnp.exp(lg - mg), axis=-1, keepdims=True)
    g_idx = g_lane.astype(I32) - N_EXPERTS
    in_group = (lane < N_EXPERTS) & (lax.shift_right_logical(lane, 3) == g_idx)
    le = jnp.where(in_group, logits, -jnp.inf)
    l1 = jnp.max(le, axis=-1, keepdims=True)
    e1 = jnp.min(jnp.where(le == l1, lane_f, float(LANES)), axis=-1, keepdims=True)
    hot1 = lane_f == e1
    le2 = jnp.where(hot1, -jnp.inf, le)
    l2 = jnp.max(le2, axis=-1, keepdims=True)
    e2 = jnp.min(jnp.where(le2 == l2, lane_f, float(LANES)), axis=-1, keepdims=True)
    hot2 = lane_f == e2
    e21 = jnp.exp(l2 - l1)
    w1 = pg_top / (1.0 + e21)
    w2 = pg_top * e21 / (1.0 + e21)

    both = (hot1 | hot2).astype(BF16)
    r = lax.broadcasted_iota(I32, (tm, tm), 0)
    c = lax.broadcasted_iota(I32, (tm, tm), 1)
    before = (c < r).astype(BF16)
    prefix = jnp.dot(before, both, preferred_element_type=F32) + carry_ref[0:1, :]
    rank1 = jnp.sum(jnp.where(hot1, prefix, 0.0), axis=-1, keepdims=True)
    rank2 = jnp.sum(jnp.where(hot2, prefix, 0.0), axis=-1, keepdims=True)
    total = carry_ref[0:1, :] + jnp.sum(both.astype(F32), axis=0, keepdims=True)
    carry_ref[0:1, :] = total
    cnt_ref[...] = jnp.broadcast_to(total, cnt_ref.shape)

    info = jnp.where(lane == 0, e1, 0.0)
    for idx, val in enumerate((e2, w1, w2, rank1, rank2), start=1):
        info = jnp.where(lane == idx, val, info)
    info_ref[...] = info


def _outproj_router(ygm, yml, yda, x, wo, g2, wr, br, *, tm=512):
    n = x.shape[0]
    row = lambda w: pl.BlockSpec((tm, w), lambda i: (i, 0))
    full = lambda a, b: pl.BlockSpec((a, b), lambda i: (0, 0))
    return pl.pallas_call(
        _outproj_router_kernel,
        grid=(n // tm,),
        in_specs=[
            row(GM_W), row(ML_W), row(DA_W), row(D_MODEL),
            pl.BlockSpec((GM_W, D_MODEL), lambda i: (0, 0)),
            pl.BlockSpec((ML_W, D_MODEL), lambda i: (0, 0)),
            pl.BlockSpec((DA_W, D_MODEL), lambda i: (0, 0)),
            full(1, D_MODEL), full(D_MODEL, 2 * LANES), full(1, LANES),
        ],
        out_specs=[row(D_MODEL), row(D_MODEL // 2), row(LANES), full(8, LANES)],
        out_shape=[
            jax.ShapeDtypeStruct((n, D_MODEL), F32),
            jax.ShapeDtypeStruct((n, D_MODEL // 2), U32),
            jax.ShapeDtypeStruct((n, LANES), F32),
            jax.ShapeDtypeStruct((8, LANES), F32),
        ],
        scratch_shapes=[pltpu.VMEM((8, LANES), F32)],
        compiler_params=_cparams(("arbitrary",)),
        name="outproj_router",
    )(ygm, yml, yda, x, wo[0], wo[1], wo[2], g2, wr, br)


ROW_DMA_UNROLL = 8


def _start_row_dmas(tt, row_copy):
    def issue(r, carry):
        for k in range(2):
            row_copy(r, k).start(priority=k)
        return carry

    lax.fori_loop(0, tt, issue, 0, unroll=ROW_DMA_UNROLL)


def _wait_row_dmas(tt, row_copy):
    def drain(r, carry):
        for k in range(2):
            row_copy(r, k).wait()
        return carry

    lax.fori_loop(0, tt, drain, 0, unroll=ROW_DMA_UNROLL)


def _row_dmas(tt, row_copy):
    _start_row_dmas(tt, row_copy)
    _wait_row_dmas(tt, row_copy)


def _dispatch_kernel(pos0_ref, pos1_ref, h_ref, xs_ref, sem):
    pos = (pos0_ref, pos1_ref)
    _row_dmas(h_ref.shape[0],
              lambda r, k: pltpu.make_async_copy(h_ref.at[pl.ds(r, 1)], xs_ref.at[pl.ds(pos[k][r], 1)], sem))


def _dispatch(pos0, pos1, hp, *, tt=256):
    n, w = hp.shape
    idx = pl.BlockSpec((tt,), lambda i: (i,), memory_space=pltpu.SMEM)
    return pl.pallas_call(
        _dispatch_kernel,
        grid=(n // tt,),
        in_specs=[idx, idx, pl.BlockSpec((tt, w), lambda i: (i, 0))],
        out_specs=pl.BlockSpec(memory_space=pl.ANY),
        out_shape=jax.ShapeDtypeStruct((2 * n, w), U32),
        scratch_shapes=[pltpu.SemaphoreType.DMA(())],
        compiler_params=_cparams(("arbitrary",)),
        name="dispatch",
    )(pos0, pos1, hp)


def _gmm_kernel(tile_ref, exp_ref, lo_ref, hi_ref, first_ref, newexp_ref,
                xs_ref, wg_ref, wu_ref, wd_ref, o_ref, wg_sc, wu_sc, wd_sc):
    i = pl.program_id(0)
    tm = xs_ref.shape[0]

    @pl.when(newexp_ref[i] == 1)
    def _():
        wg_sc[...] = wg_ref[...].astype(BF16)
        wu_sc[...] = wu_ref[...].astype(BF16)
        wd_sc[...] = wd_ref[...].astype(BF16)

    @pl.when(first_ref[i] == 1)
    def _():
        o_ref[...] = jnp.zeros_like(o_ref)

    lo, hi = lo_ref[i], hi_ref[i]

    @pl.when(hi > lo)
    def _():
        half = D_MODEL // 2
        packed = xs_ref[...]
        xa = pltpu.bitcast(packed & jnp.uint32(0xFFFF0000), F32).astype(BF16)
        xb = pltpu.bitcast(packed << 16, F32).astype(BF16)
        g = (jnp.dot(xa, wg_sc[0:half, :], preferred_element_type=F32)
             + jnp.dot(xb, wg_sc[half:, :], preferred_element_type=F32))
        u = (jnp.dot(xa, wu_sc[0:half, :], preferred_element_type=F32)
             + jnp.dot(xb, wu_sc[half:, :], preferred_element_type=F32))
        hdn = (g * jax.nn.sigmoid(g) * u).astype(BF16)
        y = jnp.dot(hdn, wd_sc[...], preferred_element_type=F32)
        y_hi = pltpu.bitcast(y[:, :half].astype(BF16).astype(F32), U32)
        y_lo = pltpu.bitcast(y[:, half:].astype(BF16).astype(F32), U32)
        rows = lax.broadcasted_iota(I32, (tm, 1), 0) + tile_ref[i] * tm
        o_ref[...] = jnp.where((rows >= lo) & (rows < hi), y_hi | (y_lo >> 16), o_ref[...])


def _gmm(meta, xs, w_gate, w_up, w_down, layer, *, tm):
    rows = xs.shape[0]
    n_items = meta[0].shape[0]
    wspec = lambda a, b: pl.BlockSpec((None, None, a, b), lambda i, tile, exp, *_: (layer, exp[i], 0, 0))
    grid_spec = pltpu.PrefetchScalarGridSpec(
        num_scalar_prefetch=6,
        grid=(n_items,),
        in_specs=[
            pl.BlockSpec((tm, D_MODEL // 2), lambda i, tile, *_: (tile[i], 0)),
            wspec(D_MODEL, D_EXPERT), wspec(D_MODEL, D_EXPERT), wspec(D_EXPERT, D_MODEL),
        ],
        out_specs=pl.BlockSpec((tm, D_MODEL // 2), lambda i, tile, *_: (tile[i], 0)),
        scratch_shapes=[
            pltpu.VMEM((D_MODEL, D_EXPERT), BF16),
            pltpu.VMEM((D_MODEL, D_EXPERT), BF16),
            pltpu.VMEM((D_EXPERT, D_MODEL), BF16),
        ],
    )
    return pl.pallas_call(
        _gmm_kernel,
        grid_spec=grid_spec,
        out_shape=jax.ShapeDtypeStruct((rows, D_MODEL // 2), U32),
        compiler_params=_cparams(("arbitrary",)),
        name="gmm",
    )(*meta, xs, w_gate, w_up, w_down)


def _gmm_items(counts, *, rows, tm):
    n_items = rows // tm + N_EXPERTS - 1
    ends = jnp.cumsum(counts)
    starts = ends - counts
    first_tile = starts // tm
    n_tiles = jnp.where(counts > 0, (ends - 1) // tm - first_tile + 1, 0)
    item_end = jnp.cumsum(n_tiles)
    item_start = item_end - n_tiles
    total = item_end[-1]
    idx = jnp.arange(n_items, dtype=I32)
    valid = idx < total
    last = jnp.maximum(total - 1, 0)
    src = jnp.where(valid, idx, last)
    exp = jnp.minimum(jnp.sum((item_end[None, :] <= src[:, None]).astype(I32), axis=1), N_EXPERTS - 1)
    tile = (first_tile[exp] + src - item_start[exp]).astype(I32)
    lo = jnp.where(valid, starts[exp], 0).astype(I32)
    hi = jnp.where(valid, ends[exp], 0).astype(I32)
    prev_tile = jnp.concatenate([jnp.full((1,), -1, I32), tile[:-1]])
    prev_exp = jnp.concatenate([jnp.full((1,), -1, I32), exp[:-1]])
    first = (valid & (tile != prev_tile)).astype(I32)
    newexp = (valid & (exp != prev_exp)).astype(I32)
    return tile, exp, lo, hi, first, newexp


def _combine_kernel(pos0_ref, pos1_ref, nxt0_ref, nxt1_ref, x_ref, info_ref, ys_ref, o_ref, gbuf, sem):
    i = pl.program_id(0)
    tt = x_ref.shape[0]
    slot = i % 2

    def gather(pos, s):
        return lambda r, k: pltpu.make_async_copy(ys_ref.at[pl.ds(pos[k][r], 1)], gbuf.at[s, k, pl.ds(r, 1)], sem.at[s])

    @pl.when(i == 0)
    def _():
        _start_row_dmas(tt, gather((pos0_ref, pos1_ref), 0))

    @pl.when(i + 1 < pl.num_programs(0))
    def _():
        _start_row_dmas(tt, gather((nxt0_ref, nxt1_ref), 1 - slot))

    _wait_row_dmas(tt, gather((pos0_ref, pos1_ref), slot))
    half = D_MODEL // 2
    w = (info_ref[:, 2:3], info_ref[:, 3:4])
    upper = lambda k: pltpu.bitcast(gbuf[slot, k] & jnp.uint32(0xFFFF0000), F32)
    lower = lambda k: pltpu.bitcast(gbuf[slot, k] << 16, F32)
    o_ref[:, :half] = x_ref[:, :half] + w[0] * upper(0) + w[1] * upper(1)
    o_ref[:, half:] = x_ref[:, half:] + w[0] * lower(0) + w[1] * lower(1)


def _combine(pos0, pos1, x1, info, ys, *, tt=256):
    n = x1.shape[0]
    last = n // tt - 1
    idx = pl.BlockSpec((tt,), lambda i: (i,), memory_space=pltpu.SMEM)
    idx_next = pl.BlockSpec((tt,), lambda i: (jnp.minimum(i + 1, last),), memory_space=pltpu.SMEM)
    return pl.pallas_call(
        _combine_kernel,
        grid=(n // tt,),
        in_specs=[
            idx, idx, idx_next, idx_next,
            pl.BlockSpec((tt, D_MODEL), lambda i: (i, 0)),
            pl.BlockSpec((tt, LANES), lambda i: (i, 0)),
            pl.BlockSpec(memory_space=pl.ANY),
        ],
        out_specs=pl.BlockSpec((tt, D_MODEL), lambda i: (i, 0)),
        out_shape=jax.ShapeDtypeStruct((n, D_MODEL), F32),
        scratch_shapes=[pltpu.VMEM((2, 2, tt, D_MODEL // 2), U32), pltpu.SemaphoreType.DMA((2,))],
        compiler_params=_cparams(("arbitrary",)),
        name="combine",
    )(pos0, pos1, pos0, pos1, x1, info, ys)


def _rope_tables(seq):
    pos = jnp.arange(seq, dtype=F32)
    inv_freq = ROPE_THETA ** (-jnp.arange(0, ROPE_DIM, 2, dtype=F32) / ROPE_DIM)
    ang = pos[:, None] * inv_freq[None, :]
    cos, sin = jnp.cos(ang), jnp.sin(ang)
    half = ROPE_DIM // 2
    z = lambda w: jnp.zeros((seq, w), F32)
    rest = DA_DK - ROPE_DIM
    cos_t = jnp.concatenate([cos, cos, jnp.ones((seq, rest), F32)], axis=1)
    sa_t = jnp.concatenate([-sin, z(half), z(rest)], axis=1)
    sb_t = jnp.concatenate([z(half), sin, z(rest)], axis=1)
    tile2 = lambda t: jnp.concatenate([t, t], axis=1)
    return tile2(cos_t), tile2(sa_t), tile2(sb_t)


def kernel(x, norm1_g, w_in, gm_vnorm_g, gm_ws, gm_b, ml_conv_w, ml_conv_b, ml_ig_b, ml_fg_b, ml_norm_g, da_qnorm_g, da_knorm_g, da_lambda, da_subln_g, w_out, norm2_g, moe_w_rg, moe_b_rg, moe_w_re, moe_b_re, moe_w_gate, moe_w_up, moe_w_down):
    bsz, seq, d = x.shape
    n = bsz * seq
    depth = w_in.shape[0]
    gmm_tm = 512
    tt_rows = 256
    flash_tq = min(1024, seq)
    cos_t, sa_t, sb_t = _rope_tables(seq)
    xf = x.reshape(n, d)
    row = lambda v: v.reshape(1, -1).astype(F32)
    lane_pad = lambda v, w: jnp.concatenate([v, jnp.zeros((w - v.shape[0],), v.dtype)])
    w_main, w_gate = _wprep(w_in)
    for l in range(depth):
        lambda_init = 0.8 - 0.6 * math.exp(-0.3 * l)
        proj, gates = _inproj(xf, row(norm1_g[l]), w_main, w_gate, l)

        y_gm = _gmlp(proj, row(gm_vnorm_g[l]), gm_ws[l], gm_b[l].T)

        gate_b = jnp.concatenate([lane_pad(ml_ig_b[l], LANES), lane_pad(ml_fg_b[l], LANES)]).reshape(1, P_GATE)
        y_ml = _mlstm(proj, gates, ml_conv_w[l], row(ml_conv_b[l]), gate_b, row(ml_norm_g[l]), bsz=bsz, seq=seq)

        qg = jnp.tile(da_qnorm_g[l], 2).reshape(1, LANES)
        kg = jnp.tile(da_knorm_g[l], 2).reshape(1, LANES)
        q0t, q1t, kf, vt = _qkprep(proj, qg, kg, cos_t, sa_t, sb_t, bsz=bsz, seq=seq, tt=flash_tq, tk=flash_tq)
        y_da = _flash(q0t, q1t, kf, vt, da_lambda[l], da_subln_g[l].reshape(LANES, 1), bsz=bsz, seq=seq,
                      lambda_init=lambda_init, tq=flash_tq)

        wo = w_out[l].astype(BF16)
        wo_parts = (wo[:GM_W], wo[GM_W:GM_W + ML_W], wo[GM_W + ML_W:])
        wr = jnp.concatenate([moe_w_re[l], moe_w_rg[l], jnp.zeros((d, LANES - N_EXPERTS - N_GROUPS), F32)], axis=1)
        wr_hi = wr.astype(BF16)
        wr_lo = (wr - wr_hi.astype(F32)).astype(BF16)
        br = lane_pad(jnp.concatenate([moe_b_re[l], moe_b_rg[l]]), LANES).reshape(1, LANES)
        x1, hp, info, cnt = _outproj_router(y_gm, y_ml, y_da, xf, wo_parts, row(norm2_g[l]),
                                            jnp.concatenate([wr_hi, wr_lo], axis=1), br)

        counts = cnt[0, :N_EXPERTS].astype(I32)
        offsets = jnp.cumsum(counts) - counts
        e12 = info[:, 0:2].astype(I32)
        pos = offsets[e12] + info[:, 4:6].astype(I32)
        pos0, pos1 = pos[:, 0], pos[:, 1]
        meta = _gmm_items(counts, rows=2 * n, tm=gmm_tm)

        xs = _dispatch(pos0, pos1, hp, tt=min(1024, n))
        ys = _gmm(meta, xs, moe_w_gate, moe_w_up, moe_w_down, l, tm=gmm_tm)
        xf = _combine(pos0, pos1, x1, info, ys, tt=tt_rows)
    return xf.reshape(bsz, seq, d)
```

```python
import functools
import math

import jax
import jax.numpy as jnp
from jax import lax
from jax.experimental import pallas as pl
from jax.experimental.pallas import tpu as pltpu

F32 = jnp.float32
BF16 = jnp.bfloat16
I32 = jnp.int32
U32 = jnp.uint32
HIGHEST = lax.Precision.HIGHEST
LOG2_E = math.log2(math.e)

LANES = 128
D_MODEL = 2048
CHUNK = 128
GM_GROUPS = 4
GM_W = 512
ML_HEADS = 6
ML_DK = 64
ML_W = 768
ML_CONV = 4
DA_HEADS = 6
DA_DK = 64
DA_W = 768
ROPE_THETA = 500000.0
ROPE_DIM = 16
N_GROUPS = 4
EXP_PER_GROUP = 8
N_EXPERTS = 32
D_EXPERT = 512
EPS = 1e-6

COL_DA_Q, COL_DA_K, COL_DA_V = 0, 768, 1536
COL_ML_V, COL_ML_O, COL_ML_QK = 2304, 3072, 3840
COL_GM_U, COL_GM_V = 4608, 5120
P_MAIN = 5632
P_GATE = 256

VMEM_LIMIT = 56 * 1024 * 1024


def _cparams(sem):
    return pltpu.CompilerParams(dimension_semantics=sem, vmem_limit_bytes=VMEM_LIMIT)


def _gelu(x):
    return 0.5 * x * (1.0 + lax.erf(x * (1.0 / math.sqrt(2.0))))


def _rms(x, g):
    return x * lax.rsqrt(jnp.mean(x * x, axis=-1, keepdims=True) + EPS) * g


_W_IN_MOVES = (
    (3340, 2304, COL_DA_Q),
    (1792, 1536, COL_ML_V),
    (1024, 768, COL_ML_QK),
    (0, 1024, COL_GM_U),
)
_W_IN_IG, _W_IN_FG = 3328, 3334


def _wprep_kernel(w_ref, main_ref, gate_ref):
    for src, width, dst in _W_IN_MOVES:
        main_ref[:, dst:dst + width] = w_ref[:, src:src + width].astype(BF16)
    gate_ref[...] = jnp.zeros_like(gate_ref)
    gate_ref[:, 0:ML_HEADS] = w_ref[:, _W_IN_IG:_W_IN_IG + ML_HEADS].astype(BF16)
    gate_ref[:, LANES:LANES + ML_HEADS] = w_ref[:, _W_IN_FG:_W_IN_FG + ML_HEADS].astype(BF16)


def _wprep(w_in, *, tr=256):
    depth, d, d_in = w_in.shape
    return pl.pallas_call(
        _wprep_kernel,
        grid=(depth, d // tr),
        in_specs=[pl.BlockSpec((None, tr, d_in), lambda l, i: (l, i, 0))],
        out_specs=[
            pl.BlockSpec((None, tr, P_MAIN), lambda l, i: (l, i, 0)),
            pl.BlockSpec((None, tr, P_GATE), lambda l, i: (l, i, 0)),
        ],
        out_shape=[
            jax.ShapeDtypeStruct((depth, d, P_MAIN), BF16),
            jax.ShapeDtypeStruct((depth, d, P_GATE), BF16),
        ],
        compiler_params=_cparams(("parallel", "parallel")),
        name="wprep",
    )(w_in)


def _inproj_kernel(x_ref, g_ref, w_ref, wg_ref, o_ref, og_ref):
    h = _rms(x_ref[...], g_ref[...]).astype(BF16)
    og_ref[...] = jnp.dot(h, wg_ref[...], preferred_element_type=F32)
    o_ref[...] = jnp.dot(h, w_ref[...], preferred_element_type=F32).astype(BF16)


def _inproj(x, g, w_main, w_gate, layer, *, tm=512):
    n = x.shape[0]
    resident = dict(pipeline_mode=pl.Buffered(1))
    return pl.pallas_call(
        _inproj_kernel,
        grid=(n // tm,),
        in_specs=[
            pl.BlockSpec((tm, D_MODEL), lambda i: (i, 0)),
            pl.BlockSpec((1, D_MODEL), lambda i: (0, 0)),
            pl.BlockSpec((None, D_MODEL, P_MAIN), lambda i: (layer, 0, 0), **resident),
            pl.BlockSpec((None, D_MODEL, P_GATE), lambda i: (layer, 0, 0), **resident),
        ],
        out_specs=[
            pl.BlockSpec((tm, P_MAIN), lambda i: (i, 0)),
            pl.BlockSpec((tm, P_GATE), lambda i: (i, 0)),
        ],
        out_shape=[
            jax.ShapeDtypeStruct((n, P_MAIN), BF16),
            jax.ShapeDtypeStruct((n, P_GATE), F32),
        ],
        compiler_params=_cparams(("parallel",)),
        name="inproj",
    )(x, g, w_main, w_gate)


def _gmlp_kernel(u_ref, v_ref, vg_ref, ws_ref, bt_ref, o_ref):
    tt = u_ref.shape[0]
    row = lax.broadcasted_iota(I32, (CHUNK, CHUNK), 0)
    col = lax.broadcasted_iota(I32, (CHUNK, CHUNK), 1)
    causal = col <= row
    for g in range(GM_GROUPS):
        cs = slice(g * LANES, (g + 1) * LANES)
        wsg = jnp.where(causal, ws_ref[g], 0.0).astype(BF16)
        bcol = bt_ref[:, g:g + 1]
        for c in range(tt // CHUNK):
            rs = slice(c * CHUNK, (c + 1) * CHUNK)
            v = _gelu(v_ref[rs, cs].astype(F32))
            vn = _rms(v, vg_ref[:, cs]).astype(BF16)
            z = jnp.dot(wsg, vn, preferred_element_type=F32) + bcol
            u = _gelu(u_ref[rs, cs].astype(F32))
            o_ref[rs, cs] = (u * z).astype(BF16)


def _gmlp(proj, vnorm_g, ws, b_t, *, tt=512):
    n = proj.shape[0]
    return pl.pallas_call(
        _gmlp_kernel,
        grid=(n // tt,),
        in_specs=[
            pl.BlockSpec((tt, GM_W), lambda i: (i, COL_GM_U // GM_W)),
            pl.BlockSpec((tt, GM_W), lambda i: (i, COL_GM_V // GM_W)),
            pl.BlockSpec((1, GM_W), lambda i: (0, 0)),
            pl.BlockSpec((GM_GROUPS, CHUNK, CHUNK), lambda i: (0, 0, 0)),
            pl.BlockSpec((CHUNK, GM_GROUPS), lambda i: (0, 0)),
        ],
        out_specs=pl.BlockSpec((tt, GM_W), lambda i: (i, 0)),
        out_shape=jax.ShapeDtypeStruct((n, GM_W), BF16),
        compiler_params=_cparams(("parallel",)),
        name="gmlp",
    )(proj, proj, vnorm_g, ws, b_t)


def _mlstm_kernel(qk_ref, v_ref, o_ref, gt_ref, cw_ref, cb_ref, gb_ref, ng_ref, out_ref,
                  xx_ref, c_ref, n_ref, m_ref):
    @pl.when(pl.program_id(1) == 0)
    def _():
        xx_ref[0:8, :] = jnp.zeros((8, ML_W), F32)
        c_ref[...] = jnp.zeros_like(c_ref)
        n_ref[...] = jnp.zeros_like(n_ref)
        m_ref[...] = jnp.zeros_like(m_ref)

    x = qk_ref[...].astype(F32)
    xx_ref[8:8 + CHUNK, :] = x
    conv = cb_ref[...]
    for j in range(ML_CONV):
        conv = conv + cw_ref[j:j + 1, :] * xx_ref[5 + j:5 + j + CHUNK, :]
    xx_ref[0:8, :] = x[CHUNK - 8:CHUNK, :]
    qk = conv * jax.nn.sigmoid(conv)

    gates = gt_ref[...] + gb_ref[...]
    ig = gates[:, :LANES]
    lf = jax.nn.log_sigmoid(gates[:, LANES:])
    row = lax.broadcasted_iota(I32, (CHUNK, CHUNK), 0)
    col = lax.broadcasted_iota(I32, (CHUNK, CHUNK), 1)
    causal = col <= row
    bcum = jnp.dot(causal.astype(F32), lf, precision=HIGHEST, preferred_element_type=F32)
    b_end = bcum[CHUNK - 1:CHUNK, :]
    m_st = m_ref[...]
    inter_all = bcum + m_st
    r_all = ig - bcum
    r_t = r_all.T
    w_log = b_end + r_all
    m_new = jnp.maximum(b_end + m_st, jnp.max(w_log, axis=0, keepdims=True))
    w_upd_all = jnp.exp(w_log - m_new)
    decay_all = jnp.exp(b_end + m_st - m_new)
    m_ref[...] = m_new

    lane = lax.broadcasted_iota(I32, (1, LANES), 1)
    sub = lax.broadcasted_iota(I32, (LANES, 1), 0)
    for h in range(ML_HEADS):
        j, half = divmod(h, 2)
        lo, hi = ML_DK * half, ML_DK * (half + 1)
        lane_sel = (lane >= lo) & (lane < hi)
        row_sel = (sub >= lo) & (sub < hi)
        hs = slice(h * LANES, (h + 1) * LANES)
        q_f = jnp.where(lane_sel, qk[:, j * LANES:(j + 1) * LANES], 0.0)
        k_f = jnp.where(lane_sel, qk[:, ML_HEADS * ML_DK + j * LANES:ML_HEADS * ML_DK + (j + 1) * LANES], 0.0) * (ML_DK ** -0.5)
        q_b = q_f.astype(BF16)
        b_col = bcum[:, h:h + 1]
        inter = inter_all[:, h:h + 1]
        d = jnp.where(causal, b_col + r_t[h:h + 1, :], -jnp.inf)
        m_t = jnp.maximum(inter, jnp.max(d, axis=-1, keepdims=True))
        w_intra = jnp.exp(d - m_t)
        w_inter = jnp.exp(inter - m_t)
        s = lax.dot_general(q_b, k_f.astype(BF16), (((1,), (1,)), ((), ())), preferred_element_type=F32) * w_intra
        v_h = v_ref[:, hs]
        c_pair = c_ref[j]
        n_pair = n_ref[j:j + 1, :]
        num = jnp.dot(s.astype(BF16), v_h, preferred_element_type=F32) + w_inter * jnp.dot(
            q_b, c_pair.astype(BF16), preferred_element_type=F32)
        den = jnp.sum(s, axis=-1, keepdims=True) + w_inter * jnp.sum(q_f * n_pair, axis=-1, keepdims=True)
        hh = num / jnp.maximum(jnp.abs(den), jnp.exp(-m_t))
        kw = k_f * w_upd_all[:, h:h + 1]
        decay = decay_all[:, h:h + 1]
        upd = lax.dot_general(kw.astype(BF16), v_h, (((0,), (0,)), ((), ())), preferred_element_type=F32)
        c_ref[j] = jnp.where(row_sel, decay * c_pair + upd, c_pair)
        n_ref[j:j + 1, :] = jnp.where(lane_sel, decay * n_pair + jnp.sum(kw, axis=0, keepdims=True), n_pair)
        hn = _rms(hh, ng_ref[:, hs])
        out_ref[:, hs] = (jax.nn.sigmoid(o_ref[:, hs].astype(F32)) * hn).astype(BF16)


def _mlstm(proj, gates, conv_w, conv_b, gate_b, norm_g, *, bsz, seq):
    n = proj.shape[0]
    nc = seq // CHUNK
    return pl.pallas_call(
        _mlstm_kernel,
        grid=(bsz, nc),
        in_specs=[
            pl.BlockSpec((CHUNK, ML_W), lambda b, c: (b * nc + c, COL_ML_QK // ML_W)),
            pl.BlockSpec((CHUNK, ML_W), lambda b, c: (b * nc + c, COL_ML_V // ML_W)),
            pl.BlockSpec((CHUNK, ML_W), lambda b, c: (b * nc + c, COL_ML_O // ML_W)),
            pl.BlockSpec((CHUNK, P_GATE), lambda b, c: (b * nc + c, 0)),
            pl.BlockSpec((ML_CONV, ML_W), lambda b, c: (0, 0)),
            pl.BlockSpec((1, ML_W), lambda b, c: (0, 0)),
            pl.BlockSpec((1, P_GATE), lambda b, c: (0, 0)),
            pl.BlockSpec((1, ML_W), lambda b, c: (0, 0)),
        ],
        out_specs=pl.BlockSpec((CHUNK, ML_W), lambda b, c: (b * nc + c, 0)),
        out_shape=jax.ShapeDtypeStruct((n, ML_W), BF16),
        scratch_shapes=[
            pltpu.VMEM((8 + CHUNK, ML_W), F32),
            pltpu.VMEM((ML_HEADS // 2, LANES, LANES), F32),
            pltpu.VMEM((8, LANES), F32),
            pltpu.VMEM((1, LANES), F32),
        ],
        compiler_params=_cparams(("parallel", "arbitrary")),
        name="mlstm",
    )(proj, proj, proj, gates, conv_w, conv_b, gate_b, norm_g)


def _qkprep_kernel(q_ref, k_ref, v_ref, qg_ref, kg_ref, cos_ref, sa_ref, sb_ref, q0t_ref, q1t_ref, ko_ref, vt_ref, *, tk):
    tt = q_ref.shape[0]
    lane = lax.broadcasted_iota(I32, (1, LANES), 1)
    map0 = lane < DA_DK
    r = lax.shift_right_logical(lax.broadcasted_iota(I32, (2 * LANES, LANES), 0), 6) & 1
    c = lax.shift_right_logical(lax.broadcasted_iota(I32, (2 * LANES, LANES), 1), 6)
    group_sum = (r == c).astype(BF16)
    cos, sa, sb = cos_ref[...], sa_ref[...], sb_ref[...]

    def norm_rope(x, g):
        sq = x * x
        sq_hi = sq.astype(BF16)
        sq_lo = (sq - sq_hi.astype(F32)).astype(BF16)
        ss = jnp.dot(jnp.concatenate([sq_hi, sq_lo], axis=1), group_sum, preferred_element_type=F32)
        xn = x * lax.rsqrt(ss * (1.0 / DA_DK) + EPS) * g
        return xn * cos + pltpu.roll(xn, LANES - ROPE_DIM // 2, 1) * sa + pltpu.roll(xn, ROPE_DIM // 2, 1) * sb

    for h in range(DA_HEADS):
        hs = slice(h * LANES, (h + 1) * LANES)
        q = norm_rope(q_ref[:, hs].astype(F32), qg_ref[...]) * (DA_DK ** -0.5 * LOG2_E)
        q0t_ref[hs, :] = jnp.where(map0, q, 0.0).T.astype(BF16)
        q1t_ref[hs, :] = jnp.where(map0, 0.0, q).T.astype(BF16)
        ko_ref[:, hs] = norm_rope(k_ref[:, hs].astype(F32), kg_ref[...]).astype(BF16)
        vt = v_ref[:, hs].astype(F32).T.astype(BF16)
        for j in range(tt // tk):
            vt_ref[h, j] = vt[:, j * tk:(j + 1) * tk]


def _qkprep(proj, qg, kg, cos_t, sa_t, sb_t, *, bsz, seq, tt, tk):
    n = proj.shape[0]
    ns = seq // tt
    tok_in = lambda col: pl.BlockSpec((tt, DA_W), lambda b, s: (b * ns + s, col // DA_W))
    qt = pl.BlockSpec((DA_W, tt), lambda b, s: (b, s))
    tab = pl.BlockSpec((tt, LANES), lambda b, s: (s, 0))
    vec = pl.BlockSpec((1, LANES), lambda b, s: (0, 0))
    return pl.pallas_call(
        functools.partial(_qkprep_kernel, tk=tk),
        grid=(bsz, ns),
        in_specs=[tok_in(COL_DA_Q), tok_in(COL_DA_K), tok_in(COL_DA_V), vec, vec, tab, tab, tab],
        out_specs=[
            qt, qt,
            pl.BlockSpec((tt, DA_W), lambda b, s: (b * ns + s, 0)),
            pl.BlockSpec((None, DA_HEADS, tt // tk, LANES, tk), lambda b, s: (b, 0, s, 0, 0)),
        ],
        out_shape=[
            jax.ShapeDtypeStruct((bsz * DA_W, seq), BF16),
            jax.ShapeDtypeStruct((bsz * DA_W, seq), BF16),
            jax.ShapeDtypeStruct((n, DA_W), BF16),
            jax.ShapeDtypeStruct((bsz, DA_HEADS, seq // tk, LANES, tk), BF16),
        ],
        compiler_params=_cparams(("parallel", "parallel")),
        name="qkprep",
    )(proj, proj, proj, qg, kg, cos_t, sa_t, sb_t)


def _flash_kernel(q0t_ref, q1t_ref, k_ref, vt_ref, lam_ref, sg_ref, o_ref, m_sc, l_sc, a_sc, acc_sc, p_sc,
                  *, tq, lambda_init):
    tk = tq
    nq = k_ref.shape[0] // tq
    lp = lam_ref[...]
    lam = (jnp.exp(jnp.sum(lp[0:1, :] * lp[1:2, :], axis=-1, keepdims=True))
           - jnp.exp(jnp.sum(lp[2:3, :] * lp[3:4, :], axis=-1, keepdims=True)) + lambda_init)
    kpos = lax.broadcasted_iota(I32, (tk, 1), 0)
    q1 = lax.broadcasted_iota(I32, (1, tq), 1)
    causal = kpos <= jnp.concatenate([q1, q1], axis=1)

    def softmax(s, m_prev, l_prev):
        m_new = jnp.maximum(m_prev, jnp.max(s, axis=0, keepdims=True))
        alpha = jnp.exp2(m_prev - m_new)
        p = jnp.exp2(s - m_new)
        return p.astype(BF16), alpha, m_new, alpha * l_prev + jnp.sum(p, axis=0, keepdims=True)

    for qi in range(nq):
        qs = slice(qi * tq, (qi + 1) * tq)
        qt = jnp.concatenate([q0t_ref[:, qs], q1t_ref[:, qs]], axis=1)

        def scores(j, qt=qt):
            start = pl.multiple_of(j * tk, tk)
            return jnp.dot(k_ref[pl.ds(start, tk), :], qt, preferred_element_type=F32)

        if qi == 0:
            s_last = jnp.where(causal, scores(0), -jnp.inf)
            m_fin = jnp.max(s_last, axis=0, keepdims=True)
            p = jnp.exp2(s_last - m_fin)
            l_fin = jnp.sum(p, axis=0, keepdims=True)
            acc = jnp.dot(vt_ref[0], p.astype(BF16), preferred_element_type=F32)
        else:
            s0 = scores(0)
            m0 = jnp.max(s0, axis=0, keepdims=True)
            p0 = jnp.exp2(s0 - m0)
            m_sc[...] = m0
            l_sc[...] = jnp.sum(p0, axis=0, keepdims=True)
            p_sc[...] = p0.astype(BF16)
            a_sc[...] = jnp.ones_like(a_sc)
            acc_sc[...] = jnp.zeros_like(acc_sc)

            def body(j, s_cur, scores=scores):
                s_next = scores(j + 1)
                acc_sc[...] = a_sc[...] * acc_sc[...] + jnp.dot(vt_ref[j - 1], p_sc[...], preferred_element_type=F32)
                p, alpha, m_new, l_new = softmax(s_cur, m_sc[...], l_sc[...])
                p_sc[...] = p
                a_sc[...] = alpha
                m_sc[...] = m_new
                l_sc[...] = l_new
                return s_next

            s_last = lax.fori_loop(1, qi, body, scores(1))
            acc_prev = a_sc[...] * acc_sc[...] + jnp.dot(vt_ref[qi - 1], p_sc[...], preferred_element_type=F32)
            p, alpha, _, l_fin = softmax(jnp.where(causal, s_last, -jnp.inf), m_sc[...], l_sc[...])
            acc = alpha * acc_prev + jnp.dot(vt_ref[qi], p, preferred_element_type=F32)

        o = acc / l_fin
        diff = o[:, :tq] - lam * o[:, tq:]
        y = diff * lax.rsqrt(jnp.mean(diff * diff, axis=0, keepdims=True) + EPS) * sg_ref[...] * (1.0 - lambda_init)
        o_ref[qs, :] = y.T.astype(BF16)


def _flash(q0t, q1t, kf, vt, lam_p, subln_col, *, bsz, seq, lambda_init, tq):
    n = bsz * seq
    qspec = pl.BlockSpec((LANES, seq), lambda b, h: (b * DA_HEADS + h, 0))
    return pl.pallas_call(
        functools.partial(_flash_kernel, tq=tq, lambda_init=lambda_init),
        grid=(bsz, DA_HEADS),
        in_specs=[
            qspec, qspec,
            pl.BlockSpec((seq, LANES), lambda b, h: (b, h)),
            pl.BlockSpec((None, None, seq // tq, LANES, tq), lambda b, h: (b, h, 0, 0, 0)),
            pl.BlockSpec((4, DA_DK), lambda b, h: (0, 0)),
            pl.BlockSpec((LANES, 1), lambda b, h: (0, 0)),
        ],
        out_specs=pl.BlockSpec((seq, LANES), lambda b, h: (b, h)),
        out_shape=jax.ShapeDtypeStruct((n, DA_W), BF16),
        scratch_shapes=[
            pltpu.VMEM((1, 2 * tq), F32),
            pltpu.VMEM((1, 2 * tq), F32),
            pltpu.VMEM((1, 2 * tq), F32),
            pltpu.VMEM((LANES, 2 * tq), F32),
            pltpu.VMEM((tq, 2 * tq), BF16),
        ],
        compiler_params=_cparams(("parallel", "parallel")),
        name="flash",
    )(q0t, q1t, kf, vt, lam_p, subln_col)


def _outproj_router_kernel(ygm_ref, yml_ref, yda_ref, x_ref, wo_gm_ref, wo_ml_ref, wo_da_ref, g2_ref, wr_ref, br_ref,
                           x1_ref, hp_ref, info_ref, cnt_ref, carry_ref):
    tm = x_ref.shape[0]

    @pl.when(pl.program_id(0) == 0)
    def _():
        carry_ref[...] = jnp.zeros_like(carry_ref)

    x1 = (x_ref[...]
          + jnp.dot(ygm_ref[...], wo_gm_ref[...], preferred_element_type=F32)
          + jnp.dot(yml_ref[...], wo_ml_ref[...], preferred_element_type=F32)
          + jnp.dot(yda_ref[...], wo_da_ref[...], preferred_element_type=F32))
    x1_ref[...] = x1
    h2 = _rms(x1, g2_ref[...])
    h2_hi = h2.astype(BF16)
    hp_ref[...] = h2

    h2_lo = (h2 - h2_hi.astype(F32)).astype(BF16)
    hw = jnp.dot(h2_hi, wr_ref[...], preferred_element_type=F32)
    logits = (hw[:, :LANES] + hw[:, LANES:]
              + jnp.dot(h2_lo, wr_ref[:, :LANES], preferred_element_type=F32) + br_ref[...])
    lane = lax.broadcasted_iota(I32, (tm, LANES), 1)
    lane_f = lane.astype(F32)
    is_g = (lane >= N_EXPERTS) & (lane < N_EXPERTS + N_GROUPS)
    lg = jnp.where(is_g, logits, -jnp.inf)
    mg = jnp.max(lg, axis=-1, keepdims=True)
    g_lane = jnp.min(jnp.where(lg == mg, lane_f, float(LANES)), axis=-1, keepdims=True)
    pg_top = 1.0 / jnp.sum(jnp.exp(lg - mg), axis=-1, keepdims=True)
    g_idx = g_lane.astype(I32) - N_EXPERTS
    in_group = (lane < N_EXPERTS) & (lax.shift_right_logical(lane, 3) == g_idx)
    le = jnp.where(in_group, logits, -jnp.inf)
    l1 = jnp.max(le, axis=-1, keepdims=True)
    e1 = jnp.min(jnp.where(le == l1, lane_f, float(LANES)), axis=-1, keepdims=True)
    hot1 = lane_f == e1
    le2 = jnp.where(hot1, -jnp.inf, le)
    l2 = jnp.max(le2, axis=-1, keepdims=True)
    e2 = jnp.min(jnp.where(le2 == l2, lane_f, float(LANES)), axis=-1, keepdims=True)
    hot2 = lane_f == e2
    e21 = jnp.exp(l2 - l1)
    w1 = pg_top / (1.0 + e21)
    w2 = pg_top * e21 / (1.0 + e21)

    both = (hot1 | hot2).astype(BF16)
    r = lax.broadcasted_iota(I32, (tm, tm), 0)
    c = lax.broadcasted_iota(I32, (tm, tm), 1)
    before = (c < r).astype(BF16)
    prefix = jnp.dot(before, both, preferred_element_type=F32) + carry_ref[0:1, :]
    rank1 = jnp.sum(jnp.where(hot1, prefix, 0.0), axis=-1, keepdims=True)
    rank2 = jnp.sum(jnp.where(hot2, prefix, 0.0), axis=-1, keepdims=True)
    total = carry_ref[0:1, :] + jnp.sum(both.astype(F32), axis=0, keepdims=True)
    carry_ref[0:1, :] = total
    cnt_ref[...] = jnp.broadcast_to(total, cnt_ref.shape)

    info = jnp.where(lane == 0, e1, 0.0)
    for idx, val in enumerate((e2, w1, w2, rank1, rank2), start=1):
        info = jnp.where(lane == idx, val, info)
    info_ref[...] = info


def _outproj_router(ygm, yml, yda, x, wo, g2, wr, br, *, tm=512):
    n = x.shape[0]
    row = lambda w: pl.BlockSpec((tm, w), lambda i: (i, 0))
    full = lambda a, b: pl.BlockSpec((a, b), lambda i: (0, 0))
    return pl.pallas_call(
        _outproj_router_kernel,
        grid=(n // tm,),
        in_specs=[
            row(GM_W), row(ML_W), row(DA_W), row(D_MODEL),
            pl.BlockSpec((GM_W, D_MODEL), lambda i: (0, 0)),
            pl.BlockSpec((ML_W, D_MODEL), lambda i: (0, 0)),
            pl.BlockSpec((DA_W, D_MODEL), lambda i: (0, 0)),
            full(1, D_MODEL), full(D_MODEL, 2 * LANES), full(1, LANES),
        ],
        out_specs=[row(D_MODEL), row(D_MODEL), row(LANES), full(8, LANES)],
        out_shape=[
            jax.ShapeDtypeStruct((n, D_MODEL), F32),
            jax.ShapeDtypeStruct((n, D_MODEL), F32),
            jax.ShapeDtypeStruct((n, LANES), F32),
            jax.ShapeDtypeStruct((8, LANES), F32),
        ],
        scratch_shapes=[pltpu.VMEM((8, LANES), F32)],
        compiler_params=_cparams(("arbitrary",)),
        name="outproj_router",
    )(ygm, yml, yda, x, wo[0], wo[1], wo[2], g2, wr, br)


ROW_DMA_UNROLL = 8


def _start_row_dmas(tt, row_copy):
    def issue(r, carry):
        for k in range(2):
            row_copy(r, k).start(priority=k)
        return carry

    lax.fori_loop(0, tt, issue, 0, unroll=ROW_DMA_UNROLL)


def _wait_row_dmas(tt, row_copy):
    def drain(r, carry):
        for k in range(2):
            row_copy(r, k).wait()
        return carry

    lax.fori_loop(0, tt, drain, 0, unroll=ROW_DMA_UNROLL)


def _row_dmas(tt, row_copy):
    _start_row_dmas(tt, row_copy)
    _wait_row_dmas(tt, row_copy)


def _dispatch_kernel(pos0_ref, pos1_ref, h_ref, xs_ref, sem):
    pos = (pos0_ref, pos1_ref)
    _row_dmas(h_ref.shape[0],
              lambda r, k: pltpu.make_async_copy(h_ref.at[pl.ds(r, 1)], xs_ref.at[pl.ds(pos[k][r], 1)], sem))


def _dispatch(pos0, pos1, hp, *, tt=256):
    n, w = hp.shape
    idx = pl.BlockSpec((tt,), lambda i: (i,), memory_space=pltpu.SMEM)
    return pl.pallas_call(
        _dispatch_kernel,
        grid=(n // tt,),
        in_specs=[idx, idx, pl.BlockSpec((tt, w), lambda i: (i, 0))],
        out_specs=pl.BlockSpec(memory_space=pl.ANY),
        out_shape=jax.ShapeDtypeStruct((2 * n, w), hp.dtype),
        scratch_shapes=[pltpu.SemaphoreType.DMA(())],
        compiler_params=_cparams(("arbitrary",)),
        name="dispatch",
    )(pos0, pos1, hp)


def _gmm_kernel(tile_ref, exp_ref, lo_ref, hi_ref, first_ref, newexp_ref,
                xs_ref, wg_ref, wu_ref, wd_ref, o_ref, wg_sc, wu_sc, wd_sc):
    i = pl.program_id(0)
    tm = xs_ref.shape[0]

    @pl.when(newexp_ref[i] == 1)
    def _():
        wg_sc[...] = wg_ref[...].astype(BF16)
        wu_sc[...] = wu_ref[...].astype(BF16)
        wd_sc[...] = wd_ref[...].astype(BF16)

    @pl.when(first_ref[i] == 1)
    def _():
        o_ref[...] = jnp.zeros_like(o_ref)

    lo, hi = lo_ref[i], hi_ref[i]

    @pl.when(hi > lo)
    def _():
        x = xs_ref[...].astype(BF16)
        g = jnp.dot(x, wg_sc[...], preferred_element_type=F32)
        u = jnp.dot(x, wu_sc[...], preferred_element_type=F32)
        hdn = (g * jax.nn.sigmoid(g) * u).astype(BF16)
        y = jnp.dot(hdn, wd_sc[...], preferred_element_type=F32)
        rows = lax.broadcasted_iota(I32, (tm, 1), 0) + tile_ref[i] * tm
        o_ref[...] = jnp.where((rows >= lo) & (rows < hi), y, o_ref[...])


def _gmm(meta, xs, w_gate, w_up, w_down, layer, *, tm):
    rows = xs.shape[0]
    n_items = meta[0].shape[0]
    wspec = lambda a, b: pl.BlockSpec((None, None, a, b), lambda i, tile, exp, *_: (layer, exp[i], 0, 0))
    grid_spec = pltpu.PrefetchScalarGridSpec(
        num_scalar_prefetch=6,
        grid=(n_items,),
        in_specs=[
            pl.BlockSpec((tm, D_MODEL), lambda i, tile, *_: (tile[i], 0)),
            wspec(D_MODEL, D_EXPERT), wspec(D_MODEL, D_EXPERT), wspec(D_EXPERT, D_MODEL),
        ],
        out_specs=pl.BlockSpec((tm, D_MODEL), lambda i, tile, *_: (tile[i], 0)),
        scratch_shapes=[
            pltpu.VMEM((D_MODEL, D_EXPERT), BF16),
            pltpu.VMEM((D_MODEL, D_EXPERT), BF16),
            pltpu.VMEM((D_EXPERT, D_MODEL), BF16),
        ],
    )
    return pl.pallas_call(
        _gmm_kernel,
        grid_spec=grid_spec,
        out_shape=jax.ShapeDtypeStruct((rows, D_MODEL), F32),
        compiler_params=_cparams(("arbitrary",)),
        name="gmm",
    )(*meta, xs, w_gate, w_up, w_down)


def _gmm_items(counts, *, rows, tm):
    n_items = rows // tm + N_EXPERTS - 1
    ends = jnp.cumsum(counts)
    starts = ends - counts
    first_tile = starts // tm
    n_tiles = jnp.where(counts > 0, (ends - 1) // tm - first_tile + 1, 0)
    item_end = jnp.cumsum(n_tiles)
    item_start = item_end - n_tiles
    total = item_end[-1]
    idx = jnp.arange(n_items, dtype=I32)
    valid = idx < total
    last = jnp.maximum(total - 1, 0)
    src = jnp.where(valid, idx, last)
    exp = jnp.minimum(jnp.sum((item_end[None, :] <= src[:, None]).astype(I32), axis=1), N_EXPERTS - 1)
    tile = (first_tile[exp] + src - item_start[exp]).astype(I32)
    lo = jnp.where(valid, starts[exp], 0).astype(I32)
    hi = jnp.where(valid, ends[exp], 0).astype(I32)
    prev_tile = jnp.concatenate([jnp.full((1,), -1, I32), tile[:-1]])
    prev_exp = jnp.concatenate([jnp.full((1,), -1, I32), exp[:-1]])
    first = (valid & (tile != prev_tile)).astype(I32)
    newexp = (valid & (exp != prev_exp)).astype(I32)
    return tile, exp, lo, hi, first, newexp


def _combine_kernel(pos0_ref, pos1_ref, nxt0_ref, nxt1_ref, x_ref, info_ref, ys_ref, o_ref, gbuf, sem):
    i = pl.program_id(0)
    tt = x_ref.shape[0]
    slot = i % 2

    def gather(pos, s):
        return lambda r, k: pltpu.make_async_copy(ys_ref.at[pl.ds(pos[k][r], 1)], gbuf.at[s, k, pl.ds(r, 1)], sem.at[s])

    @pl.when(i == 0)
    def _():
        _start_row_dmas(tt, gather((pos0_ref, pos1_ref), 0))

    @pl.when(i + 1 < pl.num_programs(0))
    def _():
        _start_row_dmas(tt, gather((nxt0_ref, nxt1_ref), 1 - slot))

    _wait_row_dmas(tt, gather((pos0_ref, pos1_ref), slot))
    o_ref[...] = x_ref[...] + info_ref[:, 2:3] * gbuf[slot, 0] + info_ref[:, 3:4] * gbuf[slot, 1]


def _combine(pos0, pos1, x1, info, ys, *, tt=256):
    n = x1.shape[0]
    last = n // tt - 1
    idx = pl.BlockSpec((tt,), lambda i: (i,), memory_space=pltpu.SMEM)
    idx_next = pl.BlockSpec((tt,), lambda i: (jnp.minimum(i + 1, last),), memory_space=pltpu.SMEM)
    return pl.pallas_call(
        _combine_kernel,
        grid=(n // tt,),
        in_specs=[
            idx, idx, idx_next, idx_next,
            pl.BlockSpec((tt, D_MODEL), lambda i: (i, 0)),
            pl.BlockSpec((tt, LANES), lambda i: (i, 0)),
            pl.BlockSpec(memory_space=pl.ANY),
        ],
        out_specs=pl.BlockSpec((tt, D_MODEL), lambda i: (i, 0)),
        out_shape=jax.ShapeDtypeStruct((n, D_MODEL), F32),
        scratch_shapes=[pltpu.VMEM((2, 2, tt, D_MODEL), F32), pltpu.SemaphoreType.DMA((2,))],
        compiler_params=_cparams(("arbitrary",)),
        name="combine",
    )(pos0, pos1, pos0, pos1, x1, info, ys)


def _rope_tables(seq):
    pos = jnp.arange(seq, dtype=F32)
    inv_freq = ROPE_THETA ** (-jnp.arange(0, ROPE_DIM, 2, dtype=F32) / ROPE_DIM)
    ang = pos[:, None] * inv_freq[None, :]
    cos, sin = jnp.cos(ang), jnp.sin(ang)
    half = ROPE_DIM // 2
    z = lambda w: jnp.zeros((seq, w), F32)
    rest = DA_DK - ROPE_DIM
    cos_t = jnp.concatenate([cos, cos, jnp.ones((seq, rest), F32)], axis=1)
    sa_t = jnp.concatenate([-sin, z(half), z(rest)], axis=1)
    sb_t = jnp.concatenate([z(half), sin, z(rest)], axis=1)
    tile2 = lambda t: jnp.concatenate([t, t], axis=1)
    return tile2(cos_t), tile2(sa_t), tile2(sb_t)


def kernel(x, norm1_g, w_in, gm_vnorm_g, gm_ws, gm_b, ml_conv_w, ml_conv_b, ml_ig_b, ml_fg_b, ml_norm_g, da_qnorm_g, da_knorm_g, da_lambda, da_subln_g, w_out, norm2_g, moe_w_rg, moe_b_rg, moe_w_re, moe_b_re, moe_w_gate, moe_w_up, moe_w_down):
    bsz, seq, d = x.shape
    n = bsz * seq
    depth = w_in.shape[0]
    gmm_tm = 256
    tt_rows = 256
    flash_tq = min(1024, seq)
    cos_t, sa_t, sb_t = _rope_tables(seq)
    xf = x.reshape(n, d)
    row = lambda v: v.reshape(1, -1).astype(F32)
    lane_pad = lambda v, w: jnp.concatenate([v, jnp.zeros((w - v.shape[0],), v.dtype)])
    w_main, w_gate = _wprep(w_in)
    for l in range(depth):
        lambda_init = 0.8 - 0.6 * math.exp(-0.3 * l)
        proj, gates = _inproj(xf, row(norm1_g[l]), w_main, w_gate, l)

        y_gm = _gmlp(proj, row(gm_vnorm_g[l]), gm_ws[l], gm_b[l].T)

        gate_b = jnp.concatenate([lane_pad(ml_ig_b[l], LANES), lane_pad(ml_fg_b[l], LANES)]).reshape(1, P_GATE)
        y_ml = _mlstm(proj, gates, ml_conv_w[l], row(ml_conv_b[l]), gate_b, row(ml_norm_g[l]), bsz=bsz, seq=seq)

        qg = jnp.tile(da_qnorm_g[l], 2).reshape(1, LANES)
        kg = jnp.tile(da_knorm_g[l], 2).reshape(1, LANES)
        q0t, q1t, kf, vt = _qkprep(proj, qg, kg, cos_t, sa_t, sb_t, bsz=bsz, seq=seq, tt=flash_tq, tk=flash_tq)
        y_da = _flash(q0t, q1t, kf, vt, da_lambda[l], da_subln_g[l].reshape(LANES, 1), bsz=bsz, seq=seq,
                      lambda_init=lambda_init, tq=flash_tq)

        wo = w_out[l].astype(BF16)
        wo_parts = (wo[:GM_W], wo[GM_W:GM_W + ML_W], wo[GM_W + ML_W:])
        wr = jnp.concatenate([moe_w_re[l], moe_w_rg[l], jnp.zeros((d, LANES - N_EXPERTS - N_GROUPS), F32)], axis=1)
        wr_hi = wr.astype(BF16)
        wr_lo = (wr - wr_hi.astype(F32)).astype(BF16)
        br = lane_pad(jnp.concatenate([moe_b_re[l], moe_b_rg[l]]), LANES).reshape(1, LANES)
        x1, hp, info, cnt = _outproj_router(y_gm, y_ml, y_da, xf, wo_parts, row(norm2_g[l]),
                                            jnp.concatenate([wr_hi, wr_lo], axis=1), br)

        counts = cnt[0, :N_EXPERTS].astype(I32)
        offsets = jnp.cumsum(counts) - counts
        e12 = info[:, 0:2].astype(I32)
        pos = offsets[e12] + info[:, 4:6].astype(I32)
        pos0, pos1 = pos[:, 0], pos[:, 1]
        meta = _gmm_items(counts, rows=2 * n, tm=gmm_tm)

        xs = _dispatch(pos0, pos1, hp, tt=min(1024, n))
        ys = _gmm(meta, xs, moe_w_gate, moe_w_up, moe_w_down, l, tm=gmm_tm)
        xf = _combine(pos0, pos1, x1, info, ys, tt=tt_rows)
    return xf.reshape(bsz, seq, d)
```

```python
import functools
import math

import jax
import jax.numpy as jnp
from jax import lax
from jax.experimental import pallas as pl
from jax.experimental.pallas import tpu as pltpu

F32 = jnp.float32
BF16 = jnp.bfloat16
I32 = jnp.int32
U32 = jnp.uint32
HIGHEST = lax.Precision.HIGHEST
LOG2_E = math.log2(math.e)

LANES = 128
ROW_SUB = 16
D_MODEL = 2048
CHUNK = 128
GM_GROUPS = 4
GM_W = 512
ML_HEADS = 6
ML_DK = 64
ML_W = 768
ML_CONV = 4
DA_HEADS = 6
DA_DK = 64
DA_W = 768
ROPE_THETA = 500000.0
ROPE_DIM = 16
N_GROUPS = 4
EXP_PER_GROUP = 8
N_EXPERTS = 32
D_EXPERT = 512
EPS = 1e-6

COL_DA_Q, COL_DA_K, COL_DA_V = 0, 768, 1536
COL_ML_V, COL_ML_O, COL_ML_QK = 2304, 3072, 3840
COL_GM_U, COL_GM_V = 4608, 5120
P_MAIN = 5632
P_GATE = 256

VMEM_LIMIT = 56 * 1024 * 1024


def _cparams(sem):
    return pltpu.CompilerParams(dimension_semantics=sem, vmem_limit_bytes=VMEM_LIMIT)


def _gelu(x):
    return 0.5 * x * (1.0 + lax.erf(x * (1.0 / math.sqrt(2.0))))


def _rms(x, g):
    return x * lax.rsqrt(jnp.mean(x * x, axis=-1, keepdims=True) + EPS) * g


_W_IN_MOVES = (
    (3340, 2304, COL_DA_Q),
    (1792, 1536, COL_ML_V),
    (1024, 768, COL_ML_QK),
    (0, 1024, COL_GM_U),
)
_W_IN_IG, _W_IN_FG = 3328, 3334


def _wprep_kernel(w_ref, main_ref, gate_ref):
    for src, width, dst in _W_IN_MOVES:
        main_ref[:, dst:dst + width] = w_ref[:, src:src + width].astype(BF16)
    gate_ref[...] = jnp.zeros_like(gate_ref)
    gate_ref[:, 0:ML_HEADS] = w_ref[:, _W_IN_IG:_W_IN_IG + ML_HEADS].astype(BF16)
    gate_ref[:, LANES:LANES + ML_HEADS] = w_ref[:, _W_IN_FG:_W_IN_FG + ML_HEADS].astype(BF16)


def _wprep(w_in, *, tr=256):
    depth, d, d_in = w_in.shape
    return pl.pallas_call(
        _wprep_kernel,
        grid=(depth, d // tr),
        in_specs=[pl.BlockSpec((None, tr, d_in), lambda l, i: (l, i, 0))],
        out_specs=[
            pl.BlockSpec((None, tr, P_MAIN), lambda l, i: (l, i, 0)),
            pl.BlockSpec((None, tr, P_GATE), lambda l, i: (l, i, 0)),
        ],
        out_shape=[
            jax.ShapeDtypeStruct((depth, d, P_MAIN), BF16),
            jax.ShapeDtypeStruct((depth, d, P_GATE), BF16),
        ],
        compiler_params=_cparams(("parallel", "parallel")),
        name="wprep",
    )(w_in)


def _inproj_kernel(x_ref, g_ref, w_ref, wg_ref, o_ref, og_ref):
    h = _rms(x_ref[...], g_ref[...]).astype(BF16)
    og_ref[...] = jnp.dot(h, wg_ref[...], preferred_element_type=F32)
    o_ref[...] = jnp.dot(h, w_ref[...], preferred_element_type=F32).astype(BF16)


def _inproj(x, g, w_main, w_gate, layer, *, tm=512):
    n = x.shape[0]
    resident = dict(pipeline_mode=pl.Buffered(1))
    return pl.pallas_call(
        _inproj_kernel,
        grid=(n // tm,),
        in_specs=[
            pl.BlockSpec((tm, D_MODEL), lambda i: (i, 0)),
            pl.BlockSpec((1, D_MODEL), lambda i: (0, 0)),
            pl.BlockSpec((None, D_MODEL, P_MAIN), lambda i: (layer, 0, 0), **resident),
            pl.BlockSpec((None, D_MODEL, P_GATE), lambda i: (layer, 0, 0), **resident),
        ],
        out_specs=[
            pl.BlockSpec((tm, P_MAIN), lambda i: (i, 0)),
            pl.BlockSpec((tm, P_GATE), lambda i: (i, 0)),
        ],
        out_shape=[
            jax.ShapeDtypeStruct((n, P_MAIN), BF16),
            jax.ShapeDtypeStruct((n, P_GATE), F32),
        ],
        compiler_params=_cparams(("parallel",)),
        name="inproj",
    )(x, g, w_main, w_gate)


def _gmlp_kernel(u_ref, v_ref, vg_ref, ws_ref, bt_ref, o_ref):
    tt = u_ref.shape[0]
    row = lax.broadcasted_iota(I32, (CHUNK, CHUNK), 0)
    col = lax.broadcasted_iota(I32, (CHUNK, CHUNK), 1)
    causal = col <= row
    for g in range(GM_GROUPS):
        cs = slice(g * LANES, (g + 1) * LANES)
        wsg = jnp.where(causal, ws_ref[g], 0.0).astype(BF16)
        bcol = bt_ref[:, g:g + 1]
        for c in range(tt // CHUNK):
            rs = slice(c * CHUNK, (c + 1) * CHUNK)
            v = _gelu(v_ref[rs, cs].astype(F32))
            vn = _rms(v, vg_ref[:, cs]).astype(BF16)
            z = jnp.dot(wsg, vn, preferred_element_type=F32) + bcol
            u = _gelu(u_ref[rs, cs].astype(F32))
            o_ref[rs, cs] = (u * z).astype(BF16)


def _gmlp(proj, vnorm_g, ws, b_t, *, tt=512):
    n = proj.shape[0]
    return pl.pallas_call(
        _gmlp_kernel,
        grid=(n // tt,),
        in_specs=[
            pl.BlockSpec((tt, GM_W), lambda i: (i, COL_GM_U // GM_W)),
            pl.BlockSpec((tt, GM_W), lambda i: (i, COL_GM_V // GM_W)),
            pl.BlockSpec((1, GM_W), lambda i: (0, 0)),
            pl.BlockSpec((GM_GROUPS, CHUNK, CHUNK), lambda i: (0, 0, 0)),
            pl.BlockSpec((CHUNK, GM_GROUPS), lambda i: (0, 0)),
        ],
        out_specs=pl.BlockSpec((tt, GM_W), lambda i: (i, 0)),
        out_shape=jax.ShapeDtypeStruct((n, GM_W), BF16),
        compiler_params=_cparams(("parallel",)),
        name="gmlp",
    )(proj, proj, vnorm_g, ws, b_t)


def _mlstm_kernel(qk_ref, v_ref, o_ref, gt_ref, cw_ref, cb_ref, gb_ref, ng_ref, out_ref,
                  xx_ref, c_ref, n_ref, m_ref):
    @pl.when(pl.program_id(1) == 0)
    def _():
        xx_ref[0:8, :] = jnp.zeros((8, ML_W), F32)
        c_ref[...] = jnp.zeros_like(c_ref)
        n_ref[...] = jnp.zeros_like(n_ref)
        m_ref[...] = jnp.zeros_like(m_ref)

    x = qk_ref[...].astype(F32)
    xx_ref[8:8 + CHUNK, :] = x
    conv = cb_ref[...]
    for j in range(ML_CONV):
        conv = conv + cw_ref[j:j + 1, :] * xx_ref[5 + j:5 + j + CHUNK, :]
    xx_ref[0:8, :] = x[CHUNK - 8:CHUNK, :]
    qk = conv * jax.nn.sigmoid(conv)

    gates = gt_ref[...] + gb_ref[...]
    ig = gates[:, :LANES]
    lf = jax.nn.log_sigmoid(gates[:, LANES:])
    row = lax.broadcasted_iota(I32, (CHUNK, CHUNK), 0)
    col = lax.broadcasted_iota(I32, (CHUNK, CHUNK), 1)
    causal = col <= row
    bcum = jnp.dot(causal.astype(F32), lf, precision=HIGHEST, preferred_element_type=F32)
    b_end = bcum[CHUNK - 1:CHUNK, :]
    m_st = m_ref[...]
    inter_all = bcum + m_st
    r_all = ig - bcum
    r_t = r_all.T
    w_log = b_end + r_all
    m_new = jnp.maximum(b_end + m_st, jnp.max(w_log, axis=0, keepdims=True))
    w_upd_all = jnp.exp(w_log - m_new)
    decay_all = jnp.exp(b_end + m_st - m_new)
    m_ref[...] = m_new

    lane = lax.broadcasted_iota(I32, (1, LANES), 1)
    sub = lax.broadcasted_iota(I32, (LANES, 1), 0)
    for h in range(ML_HEADS):
        j, half = divmod(h, 2)
        lo, hi = ML_DK * half, ML_DK * (half + 1)
        lane_sel = (lane >= lo) & (lane < hi)
        row_sel = (sub >= lo) & (sub < hi)
        hs = slice(h * LANES, (h + 1) * LANES)
        q_f = jnp.where(lane_sel, qk[:, j * LANES:(j + 1) * LANES], 0.0)
        k_f = jnp.where(lane_sel, qk[:, ML_HEADS * ML_DK + j * LANES:ML_HEADS * ML_DK + (j + 1) * LANES], 0.0) * (ML_DK ** -0.5)
        q_b = q_f.astype(BF16)
        b_col = bcum[:, h:h + 1]
        inter = inter_all[:, h:h + 1]
        d = jnp.where(causal, b_col + r_t[h:h + 1, :], -jnp.inf)
        m_t = jnp.maximum(inter, jnp.max(d, axis=-1, keepdims=True))
        w_intra = jnp.exp(d - m_t)
        w_inter = jnp.exp(inter - m_t)
        s = lax.dot_general(q_b, k_f.astype(BF16), (((1,), (1,)), ((), ())), preferred_element_type=F32) * w_intra
        v_h = v_ref[:, hs]
        c_pair = c_ref[j]
        n_pair = n_ref[j:j + 1, :]
        num = jnp.dot(s.astype(BF16), v_h, preferred_element_type=F32) + w_inter * jnp.dot(
            q_b, c_pair.astype(BF16), preferred_element_type=F32)
        den = jnp.sum(s, axis=-1, keepdims=True) + w_inter * jnp.sum(q_f * n_pair, axis=-1, keepdims=True)
        hh = num / jnp.maximum(jnp.abs(den), jnp.exp(-m_t))
        kw = k_f * w_upd_all[:, h:h + 1]
        decay = decay_all[:, h:h + 1]
        upd = lax.dot_general(kw.astype(BF16), v_h, (((0,), (0,)), ((), ())), preferred_element_type=F32)
        c_ref[j] = jnp.where(row_sel, decay * c_pair + upd, c_pair)
        n_ref[j:j + 1, :] = jnp.where(lane_sel, decay * n_pair + jnp.sum(kw, axis=0, keepdims=True), n_pair)
        hn = _rms(hh, ng_ref[:, hs])
        out_ref[:, hs] = (jax.nn.sigmoid(o_ref[:, hs].astype(F32)) * hn).astype(BF16)


def _mlstm(proj, gates, conv_w, conv_b, gate_b, norm_g, *, bsz, seq):
    n = proj.shape[0]
    nc = seq // CHUNK
    return pl.pallas_call(
        _mlstm_kernel,
        grid=(bsz, nc),
        in_specs=[
            pl.BlockSpec((CHUNK, ML_W), lambda b, c: (b * nc + c, COL_ML_QK // ML_W)),
            pl.BlockSpec((CHUNK, ML_W), lambda b, c: (b * nc + c, COL_ML_V // ML_W)),
            pl.BlockSpec((CHUNK, ML_W), lambda b, c: (b * nc + c, COL_ML_O // ML_W)),
            pl.BlockSpec((CHUNK, P_GATE), lambda b, c: (b * nc + c, 0)),
            pl.BlockSpec((ML_CONV, ML_W), lambda b, c: (0, 0)),
            pl.BlockSpec((1, ML_W), lambda b, c: (0, 0)),
            pl.BlockSpec((1, P_GATE), lambda b, c: (0, 0)),
            pl.BlockSpec((1, ML_W), lambda b, c: (0, 0)),
        ],
        out_specs=pl.BlockSpec((CHUNK, ML_W), lambda b, c: (b * nc + c, 0)),
        out_shape=jax.ShapeDtypeStruct((n, ML_W), BF16),
        scratch_shapes=[
            pltpu.VMEM((8 + CHUNK, ML_W), F32),
            pltpu.VMEM((ML_HEADS // 2, LANES, LANES), F32),
            pltpu.VMEM((8, LANES), F32),
            pltpu.VMEM((1, LANES), F32),
        ],
        compiler_params=_cparams(("parallel", "arbitrary")),
        name="mlstm",
    )(proj, proj, proj, gates, conv_w, conv_b, gate_b, norm_g)


def _qkprep_kernel(q_ref, k_ref, v_ref, qg_ref, kg_ref, cos_ref, sa_ref, sb_ref, q0t_ref, q1t_ref, ko_ref, vt_ref, *, tk):
    tt = q_ref.shape[0]
    lane = lax.broadcasted_iota(I32, (1, LANES), 1)
    map0 = lane < DA_DK
    r = lax.shift_right_logical(lax.broadcasted_iota(I32, (2 * LANES, LANES), 0), 6) & 1
    c = lax.shift_right_logical(lax.broadcasted_iota(I32, (2 * LANES, LANES), 1), 6)
    group_sum = (r == c).astype(BF16)
    cos, sa, sb = cos_ref[...], sa_ref[...], sb_ref[...]

    def norm_rope(x, g):
        sq = x * x
        sq_hi = sq.astype(BF16)
        sq_lo = (sq - sq_hi.astype(F32)).astype(BF16)
        ss = jnp.dot(jnp.concatenate([sq_hi, sq_lo], axis=1), group_sum, preferred_element_type=F32)
        xn = x * lax.rsqrt(ss * (1.0 / DA_DK) + EPS) * g
        return xn * cos + pltpu.roll(xn, LANES - ROPE_DIM // 2, 1) * sa + pltpu.roll(xn, ROPE_DIM // 2, 1) * sb

    for h in range(DA_HEADS):
        hs = slice(h * LANES, (h + 1) * LANES)
        q = norm_rope(q_ref[:, hs].astype(F32), qg_ref[...]) * (DA_DK ** -0.5 * LOG2_E)
        q0t_ref[hs, :] = jnp.where(map0, q, 0.0).T.astype(BF16)
        q1t_ref[hs, :] = jnp.where(map0, 0.0, q).T.astype(BF16)
        ko_ref[:, hs] = norm_rope(k_ref[:, hs].astype(F32), kg_ref[...]).astype(BF16)
        vt = v_ref[:, hs].astype(F32).T.astype(BF16)
        for j in range(tt // tk):
            vt_ref[h, j] = vt[:, j * tk:(j + 1) * tk]


def _qkprep(proj, qg, kg, cos_t, sa_t, sb_t, *, bsz, seq, tt, tk):
    n = proj.shape[0]
    ns = seq // tt
    tok_in = lambda col: pl.BlockSpec((tt, DA_W), lambda b, s: (b * ns + s, col // DA_W))
    qt = pl.BlockSpec((DA_W, tt), lambda b, s: (b, s))
    tab = pl.BlockSpec((tt, LANES), lambda b, s: (s, 0))
    vec = pl.BlockSpec((1, LANES), lambda b, s: (0, 0))
    return pl.pallas_call(
        functools.partial(_qkprep_kernel, tk=tk),
        grid=(bsz, ns),
        in_specs=[tok_in(COL_DA_Q), tok_in(COL_DA_K), tok_in(COL_DA_V), vec, vec, tab, tab, tab],
        out_specs=[
            qt, qt,
            pl.BlockSpec((tt, DA_W), lambda b, s: (b * ns + s, 0)),
            pl.BlockSpec((None, DA_HEADS, tt // tk, LANES, tk), lambda b, s: (b, 0, s, 0, 0)),
        ],
        out_shape=[
            jax.ShapeDtypeStruct((bsz * DA_W, seq), BF16),
            jax.ShapeDtypeStruct((bsz * DA_W, seq), BF16),
            jax.ShapeDtypeStruct((n, DA_W), BF16),
            jax.ShapeDtypeStruct((bsz, DA_HEADS, seq // tk, LANES, tk), BF16),
        ],
        compiler_params=_cparams(("parallel", "parallel")),
        name="qkprep",
    )(proj, proj, proj, qg, kg, cos_t, sa_t, sb_t)


def _flash_kernel(q0t_ref, q1t_ref, k_ref, vt_ref, lam_ref, sg_ref, o_ref, m_sc, l_sc, a_sc, acc_sc, p_sc,
                  *, tq, lambda_init):
    tk = tq
    nq = k_ref.shape[0] // tq
    lp = lam_ref[...]
    lam = (jnp.exp(jnp.sum(lp[0:1, :] * lp[1:2, :], axis=-1, keepdims=True))
           - jnp.exp(jnp.sum(lp[2:3, :] * lp[3:4, :], axis=-1, keepdims=True)) + lambda_init)
    kpos = lax.broadcasted_iota(I32, (tk, 1), 0)
    q1 = lax.broadcasted_iota(I32, (1, tq), 1)
    causal = kpos <= jnp.concatenate([q1, q1], axis=1)

    def softmax(s, m_prev, l_prev):
        m_new = jnp.maximum(m_prev, jnp.max(s, axis=0, keepdims=True))
        alpha = jnp.exp2(m_prev - m_new)
        p = jnp.exp2(s - m_new)
        return p.astype(BF16), alpha, m_new, alpha * l_prev + jnp.sum(p, axis=0, keepdims=True)

    for qi in range(nq):
        qs = slice(qi * tq, (qi + 1) * tq)
        qt = jnp.concatenate([q0t_ref[:, qs], q1t_ref[:, qs]], axis=1)

        def scores(j, qt=qt):
            start = pl.multiple_of(j * tk, tk)
            return jnp.dot(k_ref[pl.ds(start, tk), :], qt, preferred_element_type=F32)

        if qi == 0:
            s_last = jnp.where(causal, scores(0), -jnp.inf)
            m_fin = jnp.max(s_last, axis=0, keepdims=True)
            p = jnp.exp2(s_last - m_fin)
            l_fin = jnp.sum(p, axis=0, keepdims=True)
            acc = jnp.dot(vt_ref[0], p.astype(BF16), preferred_element_type=F32)
        else:
            s0 = scores(0)
            m0 = jnp.max(s0, axis=0, keepdims=True)
            p0 = jnp.exp2(s0 - m0)
            m_sc[...] = m0
            l_sc[...] = jnp.sum(p0, axis=0, keepdims=True)
            p_sc[...] = p0.astype(BF16)
            a_sc[...] = jnp.ones_like(a_sc)
            acc_sc[...] = jnp.zeros_like(acc_sc)

            def body(j, s_cur, scores=scores):
                s_next = scores(j + 1)
                acc_sc[...] = a_sc[...] * acc_sc[...] + jnp.dot(vt_ref[j - 1], p_sc[...], preferred_element_type=F32)
                p, alpha, m_new, l_new = softmax(s_cur, m_sc[...], l_sc[...])
                p_sc[...] = p
                a_sc[...] = alpha
                m_sc[...] = m_new
                l_sc[...] = l_new
                return s_next

            s_last = lax.fori_loop(1, qi, body, scores(1))
            acc_prev = a_sc[...] * acc_sc[...] + jnp.dot(vt_ref[qi - 1], p_sc[...], preferred_element_type=F32)
            p, alpha, _, l_fin = softmax(jnp.where(causal, s_last, -jnp.inf), m_sc[...], l_sc[...])
            acc = alpha * acc_prev + jnp.dot(vt_ref[qi], p, preferred_element_type=F32)

        o = acc / l_fin
        diff = o[:, :tq] - lam * o[:, tq:]
        y = diff * lax.rsqrt(jnp.mean(diff * diff, axis=0, keepdims=True) + EPS) * sg_ref[...] * (1.0 - lambda_init)
        o_ref[qs, :] = y.T.astype(BF16)


def _flash(q0t, q1t, kf, vt, lam_p, subln_col, *, bsz, seq, lambda_init, tq):
    n = bsz * seq
    qspec = pl.BlockSpec((LANES, seq), lambda b, h: (b * DA_HEADS + h, 0))
    return pl.pallas_call(
        functools.partial(_flash_kernel, tq=tq, lambda_init=lambda_init),
        grid=(bsz, DA_HEADS),
        in_specs=[
            qspec, qspec,
            pl.BlockSpec((seq, LANES), lambda b, h: (b, h)),
            pl.BlockSpec((None, None, seq // tq, LANES, tq), lambda b, h: (b, h, 0, 0, 0)),
            pl.BlockSpec((4, DA_DK), lambda b, h: (0, 0)),
            pl.BlockSpec((LANES, 1), lambda b, h: (0, 0)),
        ],
        out_specs=pl.BlockSpec((seq, LANES), lambda b, h: (b, h)),
        out_shape=jax.ShapeDtypeStruct((n, DA_W), BF16),
        scratch_shapes=[
            pltpu.VMEM((1, 2 * tq), F32),
            pltpu.VMEM((1, 2 * tq), F32),
            pltpu.VMEM((1, 2 * tq), F32),
            pltpu.VMEM((LANES, 2 * tq), F32),
            pltpu.VMEM((tq, 2 * tq), BF16),
        ],
        compiler_params=_cparams(("parallel", "parallel")),
        name="flash",
    )(q0t, q1t, kf, vt, lam_p, subln_col)


def _outproj_router_kernel(ygm_ref, yml_ref, yda_ref, x_ref, wo_gm_ref, wo_ml_ref, wo_da_ref, g2_ref, wr_ref, br_ref,
                           x1_ref, hp_ref, info_ref, cnt_ref, carry_ref):
    tm = x_ref.shape[0]

    @pl.when(pl.program_id(0) == 0)
    def _():
        carry_ref[...] = jnp.zeros_like(carry_ref)

    mix = jnp.concatenate([ygm_ref[...], yml_ref[...], yda_ref[...]], axis=1)
    w_o = jnp.concatenate([wo_gm_ref[...], wo_ml_ref[...], wo_da_ref[...]], axis=0)
    x1 = x_ref[...] + jnp.dot(mix, w_o, preferred_element_type=F32)
    x1_ref[...] = x1
    h2 = _rms(x1, g2_ref[...])
    h2_hi = h2.astype(BF16)
    hp_ref[...] = h2_hi.reshape(tm, ROW_SUB, LANES)

    h2_lo = (h2 - h2_hi.astype(F32)).astype(BF16)
    hw = jnp.dot(h2_hi, wr_ref[...], preferred_element_type=F32)
    logits = (hw[:, :LANES] + hw[:, LANES:]
              + jnp.dot(h2_lo, wr_ref[:, :LANES], preferred_element_type=F32) + br_ref[...])
    lane = lax.broadcasted_iota(I32, (tm, LANES), 1)
    lane_f = lane.astype(F32)
    is_g = (lane >= N_EXPERTS) & (lane < N_EXPERTS + N_GROUPS)
    lg = jnp.where(is_g, logits, -jnp.inf)
    mg = jnp.max(lg, axis=-1, keepdims=True)
    g_lane = jnp.min(jnp.where(lg == mg, lane_f, float(LANES)), axis=-1, keepdims=True)
    pg_top = 1.0 / jnp.sum(jnp.exp(lg - mg), axis=-1, keepdims=True)
    g_idx = g_lane.astype(I32) - N_EXPERTS
    in_group = (lane < N_EXPERTS) & (lax.shift_right_logical(lane, 3) == g_idx)
    le = jnp.where(in_group, logits, -jnp.inf)
    l1 = jnp.max(le, axis=-1, keepdims=True)
    e1 = jnp.min(jnp.where(le == l1, lane_f, float(LANES)), axis=-1, keepdims=True)
    hot1 = lane_f == e1
    le2 = jnp.where(hot1, -jnp.inf, le)
    l2 = jnp.max(le2, axis=-1, keepdims=True)
    e2 = jnp.min(jnp.where(le2 == l2, lane_f, float(LANES)), axis=-1, keepdims=True)
    hot2 = lane_f == e2
    e21 = jnp.exp(l2 - l1)
    w1 = pg_top / (1.0 + e21)
    w2 = pg_top * e21 / (1.0 + e21)

    both = (hot1 | hot2).astype(BF16)
    r = lax.broadcasted_iota(I32, (tm, tm), 0)
    c = lax.broadcasted_iota(I32, (tm, tm), 1)
    before = (c < r).astype(BF16)
    prefix = jnp.dot(before, both, preferred_element_type=F32) + carry_ref[0:1, :]
    rank1 = jnp.sum(jnp.where(hot1, prefix, 0.0), axis=-1, keepdims=True)
    rank2 = jnp.sum(jnp.where(hot2, prefix, 0.0), axis=-1, keepdims=True)
    total = carry_ref[0:1, :] + jnp.sum(both.astype(F32), axis=0, keepdims=True)
    carry_ref[0:1, :] = total
    cnt_ref[...] = jnp.broadcast_to(total, cnt_ref.shape)

    info = jnp.where(lane == 0, e1, 0.0)
    for idx, val in enumerate((e2, w1, w2, rank1, rank2), start=1):
        info = jnp.where(lane == idx, val, info)
    info_ref[...] = info


def _outproj_router(ygm, yml, yda, x, wo, g2, wr, br, *, tm=512):
    n = x.shape[0]
    row = lambda w: pl.BlockSpec((tm, w), lambda i: (i, 0))
    full = lambda a, b: pl.BlockSpec((a, b), lambda i: (0, 0))
    return pl.pallas_call(
        _outproj_router_kernel,
        grid=(n // tm,),
        in_specs=[
            row(GM_W), row(ML_W), row(DA_W), row(D_MODEL),
            pl.BlockSpec((GM_W, D_MODEL), lambda i: (0, 0)),
            pl.BlockSpec((ML_W, D_MODEL), lambda i: (0, 0)),
            pl.BlockSpec((DA_W, D_MODEL), lambda i: (0, 0)),
            full(1, D_MODEL), full(D_MODEL, 2 * LANES), full(1, LANES),
        ],
        out_specs=[row(D_MODEL), pl.BlockSpec((tm, ROW_SUB, LANES), lambda i: (i, 0, 0)), row(LANES), full(8, LANES)],
        out_shape=[
            jax.ShapeDtypeStruct((n, D_MODEL), F32),
            jax.ShapeDtypeStruct((n, ROW_SUB, LANES), BF16),
            jax.ShapeDtypeStruct((n, LANES), F32),
            jax.ShapeDtypeStruct((8, LANES), F32),
        ],
        scratch_shapes=[pltpu.VMEM((8, LANES), F32)],
        compiler_params=_cparams(("arbitrary",)),
        name="outproj_router",
    )(ygm, yml, yda, x, wo[0], wo[1], wo[2], g2, wr, br)


ROW_DMA_UNROLL = 8


def _start_row_dmas(tt, row_copy):
    def issue(r, carry):
        for k in range(2):
            row_copy(r, k).start(priority=k)
        return carry

    lax.fori_loop(0, tt, issue, 0, unroll=ROW_DMA_UNROLL)


def _wait_row_dmas(tt, row_copy):
    def drain(r, carry):
        for k in range(2):
            row_copy(r, k).wait()
        return carry

    lax.fori_loop(0, tt, drain, 0, unroll=ROW_DMA_UNROLL)


def _row_dmas(tt, row_copy):
    _start_row_dmas(tt, row_copy)
    _wait_row_dmas(tt, row_copy)


def _dispatch_kernel(pos0_ref, pos1_ref, h_ref, xs_ref, sem):
    pos = (pos0_ref, pos1_ref)
    _row_dmas(h_ref.shape[0],
              lambda r, k: pltpu.make_async_copy(h_ref.at[pl.ds(r, 1)], xs_ref.at[pl.ds(pos[k][r], 1)], sem))


def _dispatch(pos0, pos1, hp, *, tt=256):
    n = hp.shape[0]
    idx = pl.BlockSpec((tt,), lambda i: (i,), memory_space=pltpu.SMEM)
    return pl.pallas_call(
        _dispatch_kernel,
        grid=(n // tt,),
        in_specs=[idx, idx, pl.BlockSpec((tt, ROW_SUB, LANES), lambda i: (i, 0, 0))],
        out_specs=pl.BlockSpec(memory_space=pl.ANY),
        out_shape=jax.ShapeDtypeStruct((2 * n, ROW_SUB, LANES), hp.dtype),
        scratch_shapes=[pltpu.SemaphoreType.DMA(())],
        compiler_params=_cparams(("arbitrary",)),
        name="dispatch",
    )(pos0, pos1, hp)


def _gmm_kernel(tile_ref, exp_ref, lo_ref, hi_ref, first_ref, newexp_ref,
                xs_ref, wg_ref, wu_ref, wd_ref, o_ref, wg_sc, wu_sc, wd_sc):
    i = pl.program_id(0)
    tm = xs_ref.shape[0]

    @pl.when(newexp_ref[i] == 1)
    def _():
        wg_sc[...] = wg_ref[...].astype(BF16)
        wu_sc[...] = wu_ref[...].astype(BF16)
        wd_sc[...] = wd_ref[...].astype(BF16)

    @pl.when(first_ref[i] == 1)
    def _():
        o_ref[...] = jnp.zeros_like(o_ref)

    lo, hi = lo_ref[i], hi_ref[i]

    @pl.when(hi > lo)
    def _():
        x = xs_ref[...].reshape(tm, D_MODEL)
        g = jnp.dot(x, wg_sc[...], preferred_element_type=F32)
        u = jnp.dot(x, wu_sc[...], preferred_element_type=F32)
        hdn = (g * jax.nn.sigmoid(g) * u).astype(BF16)
        y = jnp.dot(hdn, wd_sc[...], preferred_element_type=F32).astype(BF16).reshape(tm, ROW_SUB, LANES)
        rows = lax.broadcasted_iota(I32, (tm, 1, 1), 0) + tile_ref[i] * tm
        o_ref[...] = jnp.where((rows >= lo) & (rows < hi), y, o_ref[...])


def _gmm(meta, xs, w_gate, w_up, w_down, layer, *, tm):
    rows = xs.shape[0]
    n_items = meta[0].shape[0]
    wspec = lambda a, b: pl.BlockSpec((None, None, a, b), lambda i, tile, exp, *_: (layer, exp[i], 0, 0))
    grid_spec = pltpu.PrefetchScalarGridSpec(
        num_scalar_prefetch=6,
        grid=(n_items,),
        in_specs=[
            pl.BlockSpec((tm, ROW_SUB, LANES), lambda i, tile, *_: (tile[i], 0, 0)),
            wspec(D_MODEL, D_EXPERT), wspec(D_MODEL, D_EXPERT), wspec(D_EXPERT, D_MODEL),
        ],
        out_specs=pl.BlockSpec((tm, ROW_SUB, LANES), lambda i, tile, *_: (tile[i], 0, 0)),
        scratch_shapes=[
            pltpu.VMEM((D_MODEL, D_EXPERT), BF16),
            pltpu.VMEM((D_MODEL, D_EXPERT), BF16),
            pltpu.VMEM((D_EXPERT, D_MODEL), BF16),
        ],
    )
    return pl.pallas_call(
        _gmm_kernel,
        grid_spec=grid_spec,
        out_shape=jax.ShapeDtypeStruct((rows, ROW_SUB, LANES), BF16),
        compiler_params=_cparams(("arbitrary",)),
        name="gmm",
    )(*meta, xs, w_gate, w_up, w_down)


def _gmm_items(counts, *, rows, tm):
    n_items = rows // tm + N_EXPERTS - 1
    ends = jnp.cumsum(counts)
    starts = ends - counts
    first_tile = starts // tm
    n_tiles = jnp.where(counts > 0, (ends - 1) // tm - first_tile + 1, 0)
    item_end = jnp.cumsum(n_tiles)
    item_start = item_end - n_tiles
    total = item_end[-1]
    idx = jnp.arange(n_items, dtype=I32)
    valid = idx < total
    last = jnp.maximum(total - 1, 0)
    src = jnp.where(valid, idx, last)
    exp = jnp.minimum(jnp.sum((item_end[None, :] <= src[:, None]).astype(I32), axis=1), N_EXPERTS - 1)
    tile = (first_tile[exp] + src - item_start[exp]).astype(I32)
    lo = jnp.where(valid, starts[exp], 0).astype(I32)
    hi = jnp.where(valid, ends[exp], 0).astype(I32)
    prev_tile = jnp.concatenate([jnp.full((1,), -1, I32), tile[:-1]])
    prev_exp = jnp.concatenate([jnp.full((1,), -1, I32), exp[:-1]])
    first = (valid & (tile != prev_tile)).astype(I32)
    newexp = (valid & (exp != prev_exp)).astype(I32)
    return tile, exp, lo, hi, first, newexp


def _combine_kernel(pos0_ref, pos1_ref, nxt0_ref, nxt1_ref, x_ref, info_ref, ys_ref, o_ref, gbuf, sem):
    i = pl.program_id(0)
    tt = x_ref.shape[0]
    slot = i % 2

    def gather(pos, s):
        return lambda r, k: pltpu.make_async_copy(ys_ref.at[pl.ds(pos[k][r], 1)], gbuf.at[s, k, pl.ds(r, 1)], sem.at[s])

    @pl.when(i == 0)
    def _():
        _start_row_dmas(tt, gather((pos0_ref, pos1_ref), 0))

    @pl.when(i + 1 < pl.num_programs(0))
    def _():
        _start_row_dmas(tt, gather((nxt0_ref, nxt1_ref), 1 - slot))

    _wait_row_dmas(tt, gather((pos0_ref, pos1_ref), slot))
    rows = lambda k: gbuf[slot, k].reshape(tt, D_MODEL).astype(F32)
    o_ref[...] = x_ref[...] + info_ref[:, 2:3] * rows(0) + info_ref[:, 3:4] * rows(1)


def _combine(pos0, pos1, x1, info, ys, *, tt=256):
    n = x1.shape[0]
    last = n // tt - 1
    idx = pl.BlockSpec((tt,), lambda i: (i,), memory_space=pltpu.SMEM)
    idx_next = pl.BlockSpec((tt,), lambda i: (jnp.minimum(i + 1, last),), memory_space=pltpu.SMEM)
    return pl.pallas_call(
        _combine_kernel,
        grid=(n // tt,),
        in_specs=[
            idx, idx, idx_next, idx_next,
            pl.BlockSpec((tt, D_MODEL), lambda i: (i, 0)),
            pl.BlockSpec((tt, LANES), lambda i: (i, 0)),
            pl.BlockSpec(memory_space=pl.ANY),
        ],
        out_specs=pl.BlockSpec((tt, D_MODEL), lambda i: (i, 0)),
        out_shape=jax.ShapeDtypeStruct((n, D_MODEL), F32),
        scratch_shapes=[pltpu.VMEM((2, 2, tt, ROW_SUB, LANES), BF16), pltpu.SemaphoreType.DMA((2,))],
        compiler_params=_cparams(("arbitrary",)),
        name="combine",
    )(pos0, pos1, pos0, pos1, x1, info, ys)


def _rope_tables(seq):
    pos = jnp.arange(seq, dtype=F32)
    inv_freq = ROPE_THETA ** (-jnp.arange(0, ROPE_DIM, 2, dtype=F32) / ROPE_DIM)
    ang = pos[:, None] * inv_freq[None, :]
    cos, sin = jnp.cos(ang), jnp.sin(ang)
    half = ROPE_DIM // 2
    z = lambda w: jnp.zeros((seq, w), F32)
    rest = DA_DK - ROPE_DIM
    cos_t = jnp.concatenate([cos, cos, jnp.ones((seq, rest), F32)], axis=1)
    sa_t = jnp.concatenate([-sin, z(half), z(rest)], axis=1)
    sb_t = jnp.concatenate([z(half), sin, z(rest)], axis=1)
    tile2 = lambda t: jnp.concatenate([t, t], axis=1)
    return tile2(cos_t), tile2(sa_t), tile2(sb_t)


def kernel(x, norm1_g, w_in, gm_vnorm_g, gm_ws, gm_b, ml_conv_w, ml_conv_b, ml_ig_b, ml_fg_b, ml_norm_g, da_qnorm_g, da_knorm_g, da_lambda, da_subln_g, w_out, norm2_g, moe_w_rg, moe_b_rg, moe_w_re, moe_b_re, moe_w_gate, moe_w_up, moe_w_down):
    bsz, seq, d = x.shape
    n = bsz * seq
    depth = w_in.shape[0]
    gmm_tm = 512
    tt_rows = 256
    flash_tq = min(1024, seq)
    cos_t, sa_t, sb_t = _rope_tables(seq)
    xf = x.reshape(n, d)
    row = lambda v: v.reshape(1, -1).astype(F32)
    lane_pad = lambda v, w: jnp.concatenate([v, jnp.zeros((w - v.shape[0],), v.dtype)])
    w_main, w_gate = _wprep(w_in)
    for l in range(depth):
        lambda_init = 0.8 - 0.6 * math.exp(-0.3 * l)
        proj, gates = _inproj(xf, row(norm1_g[l]), w_main, w_gate, l)

        y_gm = _gmlp(proj, row(gm_vnorm_g[l]), gm_ws[l], gm_b[l].T)

        gate_b = jnp.concatenate([lane_pad(ml_ig_b[l], LANES), lane_pad(ml_fg_b[l], LANES)]).reshape(1, P_GATE)
        y_ml = _mlstm(proj, gates, ml_conv_w[l], row(ml_conv_b[l]), gate_b, row(ml_norm_g[l]), bsz=bsz, seq=seq)

        qg = jnp.tile(da_qnorm_g[l], 2).reshape(1, LANES)
        kg = jnp.tile(da_knorm_g[l], 2).reshape(1, LANES)
        q0t, q1t, kf, vt = _qkprep(proj, qg, kg, cos_t, sa_t, sb_t, bsz=bsz, seq=seq, tt=flash_tq, tk=flash_tq)
        y_da = _flash(q0t, q1t, kf, vt, da_lambda[l], da_subln_g[l].reshape(LANES, 1), bsz=bsz, seq=seq,
                      lambda_init=lambda_init, tq=flash_tq)

        wo = w_out[l].astype(BF16)
        wo_parts = (wo[:GM_W], wo[GM_W:GM_W + ML_W], wo[GM_W + ML_W:])
        wr = jnp.concatenate([moe_w_re[l], moe_w_rg[l], jnp.zeros((d, LANES - N_EXPERTS - N_GROUPS), F32)], axis=1)
        wr_hi = wr.astype(BF16)
        wr_lo = (wr - wr_hi.astype(F32)).astype(BF16)
        br = lane_pad(jnp.concatenate([moe_b_re[l], moe_b_rg[l]]), LANES).reshape(1, LANES)
        x1, hp, info, cnt = _outproj_router(y_gm, y_ml, y_da, xf, wo_parts, row(norm2_g[l]),
                                            jnp.concatenate([wr_hi, wr_lo], axis=1), br)

        counts = cnt[0, :N_EXPERTS].astype(I32)
        offsets = jnp.cumsum(counts) - counts
        e12 = info[:, 0:2].astype(I32)
        pos = offsets[e12] + info[:, 4:6].astype(I32)
        pos0, pos1 = pos[:, 0], pos[:, 1]
        meta = _gmm_items(counts, rows=2 * n, tm=gmm_tm)

        xs = _dispatch(pos0, pos1, hp, tt=min(1024, n))
        ys = _gmm(meta, xs, moe_w_gate, moe_w_up, moe_w_down, l, tm=gmm_tm)
        xf = _combine(pos0, pos1, x1, info, ys, tt=tt_rows)
    return xf.reshape(bsz, seq, d)
```
